```python
import math
import jax, jax.numpy as jnp
from jax import lax
import numpy as np


D_MODEL = 2048
BATCH = 4
SEQ = 8192
DEPTH = 2

EPS = 1e-6
ROPE_THETA = 10000.0

SSD_HEADS = 16
SSD_HEAD_DIM = 64
SSD_WIDTH = SSD_HEADS * SSD_HEAD_DIM
SSD_GROUPS = 2
SSD_STATE = 128
SSD_CONV = 4
SSD_CHUNK = 128
SSD_CONV_CH = SSD_WIDTH + 2 * SSD_GROUPS * SSD_STATE

ATT_HEADS = 8
ATT_HEAD_DIM = 64
ATT_WIDTH = ATT_HEADS * ATT_HEAD_DIM
DILATED_PAIRS = ((128, 1), (512, 4), (2048, 16))
ATT_BLOCK = 128

RET_HEADS = 4
RET_QK_DIM = 64
RET_V_DIM = 128
RET_QK_WIDTH = RET_HEADS * RET_QK_DIM
RET_V_WIDTH = RET_HEADS * RET_V_DIM
RET_CHUNK = 128

MIX_WIDTH = SSD_WIDTH + ATT_WIDTH + RET_V_WIDTH
IN_SPLITS = (SSD_WIDTH, SSD_CONV_CH, SSD_HEADS,
             ATT_WIDTH, ATT_WIDTH, ATT_WIDTH,
             RET_QK_WIDTH, RET_QK_WIDTH, RET_V_WIDTH, RET_V_WIDTH)
IN_WIDTH = sum(IN_SPLITS)
D_FF = ((8 * D_MODEL // 3 + 255) // 256) * 256

kernel_name = 'hybrid_ssd_dilated_retention_block'


def rms_norm(x, g):
    xf = x.astype(jnp.float32)
    y = xf * lax.rsqrt(jnp.mean(xf * xf, axis=-1, keepdims=True) + EPS)
    return (y * g.astype(jnp.float32)).astype(x.dtype)


def grouped_rms(x, groups, gain):
    b, s, w = x.shape
    xg = x.astype(jnp.float32).reshape(b, s, groups, w // groups)
    xg = xg * lax.rsqrt(jnp.mean(xg * xg, axis=-1, keepdims=True) + EPS)
    return xg.reshape(b, s, w) * gain.astype(jnp.float32)


def rope_tables(seq, dim):
    pos = jnp.arange(seq, dtype=jnp.float32)
    inv = ROPE_THETA ** (-jnp.arange(0, dim, 2, dtype=jnp.float32) / dim)
    ang = pos[:, None] * inv[None, :]
    return jnp.cos(ang), jnp.sin(ang)


def apply_rope(t, cos, sin):
    half = t.shape[-1] // 2
    t1, t2 = t[..., :half], t[..., half:]
    c = cos[None, :, None, :]
    s = sin[None, :, None, :]
    return jnp.concatenate([t1 * c - t2 * s, t1 * s + t2 * c], axis=-1)


def causal_dwconv(u, w, b):
    k = w.shape[0]
    s = u.shape[1]
    up = jnp.pad(u, ((0, 0), (k - 1, 0), (0, 0)))
    return b + sum(up[:, i:i + s] * w[i] for i in range(k))


def ssd_chunked(x, dt, a_neg, bm, cm):
    bsz, s, h, p = x.shape
    g, n = bm.shape[2], bm.shape[3]
    j = h // g
    l = SSD_CHUNK
    c = s // l
    xg = (x * dt[..., None]).reshape(bsz, c, l, g, j, p)
    da = (dt * a_neg).reshape(bsz, c, l, g, j).transpose(0, 1, 3, 4, 2)
    acum = jnp.cumsum(da, axis=-1)
    bc = bm.reshape(bsz, c, l, g, n)
    cc = cm.reshape(bsz, c, l, g, n)
    causal = jnp.tril(jnp.ones((l, l), dtype=bool))
    seg = acum[..., :, None] - acum[..., None, :]
    decay_in = jnp.exp(jnp.where(causal, seg, -jnp.inf))
    cb = jnp.einsum('bclgn,bcsgn->bcgls', cc, bc)
    y_diag = jnp.einsum('bcgjls,bcsgjp->bclgjp', cb[:, :, :, None] * decay_in, xg)
    decay_to_end = jnp.exp(acum[..., -1:] - acum)
    xw = xg * decay_to_end.transpose(0, 1, 4, 2, 3)[..., None]
    chunk_states = jnp.einsum('bclgn,bclgjp->bcgjpn', bc, xw)
    chunk_decay = jnp.exp(acum[..., -1])

    def step(state, inp):
        st_c, dec_c = inp
        return state * dec_c[..., None, None] + st_c, state

    init = jnp.zeros((bsz, g, j, p, n), jnp.float32)
    _, prev = lax.scan(step, init, (chunk_states.transpose(1, 0, 2, 3, 4, 5),
                                    chunk_decay.transpose(1, 0, 2, 3)))
    prev = prev.transpose(1, 0, 2, 3, 4, 5)
    y_off = jnp.einsum('bclgn,bcgjpn->bclgjp', cc, prev) * \
        jnp.exp(acum).transpose(0, 1, 4, 2, 3)[..., None]
    return (y_diag + y_off).reshape(bsz, s, h, p)


def dilated_branch(q, k, v, window, dilation):
    bsz, s, h, hd = q.shape
    steps = window // dilation
    blk = ATT_BLOCK
    L = s // dilation
    nb = -(-L // blk)
    pad = nb * blk - L

    def to_sub(t):
        t = t.reshape(bsz, L, dilation, h, hd).transpose(0, 2, 3, 1, 4)
        return jnp.pad(t, ((0, 0), (0, 0), (0, 0), (0, pad), (0, 0)))

    def key_blocks(t):
        t = jnp.pad(to_sub(t), ((0, 0), (0, 0), (0, 0), (blk, 0), (0, 0)))
        t = t.reshape(bsz, dilation, h, nb + 1, blk, hd)
        return jnp.concatenate([t[:, :, :, :-1], t[:, :, :, 1:]], axis=4)

    qb = to_sub(q).reshape(bsz, dilation, h, nb, blk, hd)
    kb = key_blocks(k)
    vb = key_blocks(v)
    sc = jnp.einsum('brhnqd,brhnkd->brhnqk', qb, kb).astype(jnp.float32)
    iq = jnp.arange(blk)[:, None]
    ik = jnp.arange(2 * blk)[None, :]
    dist = blk + iq - ik
    kpos = (jnp.arange(nb) * blk - blk)[:, None, None] + ik[None]
    valid = (dist >= 0) & (dist <= steps) & (kpos >= 0)
    sc = jnp.where(valid, sc, -jnp.inf)
    mx = jnp.max(sc, axis=-1, keepdims=True)
    pr = jnp.exp(sc - mx)
    den = jnp.sum(pr, axis=-1, keepdims=True)
    out = jnp.einsum('brhnqk,brhnkd->brhnqd', pr, vb.astype(jnp.float32)) / den
    lse = (mx + jnp.log(den))[..., 0]

    def from_sub(t):
        t = t.reshape(bsz, dilation, h, nb * blk, *t.shape[5:])[:, :, :, :L]
        t = jnp.moveaxis(t, 3, 1)
        return t.reshape(bsz, s, h, *t.shape[4:])

    return from_sub(out), from_sub(lse)


def retention_chunked(q, k, v):
    bsz, s, h, dk = q.shape
    dv = v.shape[-1]
    l = RET_CHUNK
    c = s // l
    log_gamma = jnp.log1p(-jnp.exp2(-5.0 - jnp.arange(h, dtype=jnp.float32)))
    idx = jnp.arange(l, dtype=jnp.float32)
    rel = idx[:, None] - idx[None, :]
    decay_in = jnp.where(rel >= 0,
                         jnp.exp(jnp.maximum(rel, 0.0)[None] * log_gamma[:, None, None]),
                         0.0)
    qc = q.reshape(bsz, c, l, h, dk)
    kc = k.reshape(bsz, c, l, h, dk)
    vc = v.reshape(bsz, c, l, h, dv)
    scores = jnp.einsum('bclhd,bcshd->bchls', qc, kc) * decay_in
    inner = jnp.einsum('bchls,bcshe->bclhe', scores, vc)
    k_to_end = jnp.exp((l - 1 - idx)[:, None] * log_gamma[None, :])
    kv = jnp.einsum('bclhd,bclhe->bchde', kc * k_to_end[:, :, None], vc)
    chunk_decay = jnp.exp(l * log_gamma)

    def step(state, kv_c):
        return state * chunk_decay[:, None, None] + kv_c, state

    init = jnp.zeros((bsz, h, dk, dv), jnp.float32)
    _, prev = lax.scan(step, init, kv.transpose(1, 0, 2, 3, 4))
    prev = prev.transpose(1, 0, 2, 3, 4)
    q_from_start = jnp.exp((idx + 1.0)[:, None] * log_gamma[None, :])
    cross = jnp.einsum('bclhd,bchde->bclhe', qc * q_from_start[:, :, None], prev)
    return (inner + cross).reshape(bsz, s, h, dv)


def hybrid_mixer(hn, w_in, conv_w, conv_b, dt_bias, a_log, d_skip, ssd_norm,
                 q_norm, k_norm, ret_norm, w_out, cos, sin):
    bsz, s, _ = hn.shape
    f32 = jnp.float32
    proj = (hn @ w_in).astype(f32)
    cuts = list(np.cumsum(IN_SPLITS)[:-1])
    z, xbc, dt_raw, aq, ak, av, rq, rk, rv, rg = jnp.split(proj, cuts, axis=-1)

    xbc = jax.nn.silu(causal_dwconv(xbc, conv_w.astype(f32), conv_b.astype(f32)))
    xs, bm, cm = jnp.split(xbc, [SSD_WIDTH, SSD_WIDTH + SSD_GROUPS * SSD_STATE], axis=-1)
    xs = xs.reshape(bsz, s, SSD_HEADS, SSD_HEAD_DIM)
    bm = bm.reshape(bsz, s, SSD_GROUPS, SSD_STATE)
    cm = cm.reshape(bsz, s, SSD_GROUPS, SSD_STATE)
    dt = jax.nn.softplus(dt_raw + dt_bias.astype(f32))
    a_neg = -jnp.exp(a_log.astype(f32))
    y_ssd = ssd_chunked(xs, dt, a_neg, bm, cm) + xs * d_skip.astype(f32)[:, None]
    y_ssd = grouped_rms(y_ssd.reshape(bsz, s, SSD_WIDTH) * jax.nn.silu(z), SSD_GROUPS, ssd_norm)

    aq = apply_rope(rms_norm(aq.reshape(bsz, s, ATT_HEADS, ATT_HEAD_DIM), q_norm), cos, sin)
    aq = aq * (ATT_HEAD_DIM ** -0.5)
    ak = apply_rope(rms_norm(ak.reshape(bsz, s, ATT_HEADS, ATT_HEAD_DIM), k_norm), cos, sin)
    av = av.reshape(bsz, s, ATT_HEADS, ATT_HEAD_DIM)
    branches = [dilated_branch(aq, ak, av, w, d) for (w, d) in DILATED_PAIRS]
    outs = jnp.stack([o for o, _ in branches])
    wts = jax.nn.softmax(jnp.stack([lse for _, lse in branches]), axis=0)
    y_att = jnp.sum(wts[..., None] * outs, axis=0).reshape(bsz, s, ATT_WIDTH)

    rq = apply_rope(rq.reshape(bsz, s, RET_HEADS, RET_QK_DIM), cos, sin)
    rk = apply_rope(rk.reshape(bsz, s, RET_HEADS, RET_QK_DIM), cos, sin) * (RET_QK_DIM ** -0.5)
    rv = rv.reshape(bsz, s, RET_HEADS, RET_V_DIM)
    y_ret = retention_chunked(rq, rk, rv).reshape(bsz, s, RET_V_WIDTH)
    y_ret = grouped_rms(y_ret, RET_HEADS, ret_norm) * jax.nn.silu(rg)

    y = jnp.concatenate([y_ssd, y_att, y_ret], axis=-1).astype(hn.dtype)
    return y @ w_out


def swiglu(hn, w_gate, w_up, w_down):
    return (jax.nn.silu(hn @ w_gate) * (hn @ w_up)) @ w_down


def setup_inputs(seed: int = 0) -> dict:
    key = jax.random.key(seed)
    ks = jax.random.split(key, 17)
    f32 = jnp.float32

    def nrm(k, shape, scale):
        return jax.random.normal(k, shape, f32) * scale

    x = nrm(ks[0], (BATCH, SEQ, D_MODEL), 1.0)
    ln_mix = 1.0 + nrm(ks[1], (DEPTH, D_MODEL), 0.02)
    w_in = nrm(ks[2], (DEPTH, D_MODEL, IN_WIDTH), D_MODEL ** -0.5)
    conv_w = nrm(ks[3], (DEPTH, SSD_CONV, SSD_CONV_CH), SSD_CONV ** -0.5)
    conv_b = nrm(ks[4], (DEPTH, SSD_CONV_CH), 0.01)
    dt0 = jnp.exp(jax.random.uniform(ks[5], (DEPTH, SSD_HEADS), f32,
                                     math.log(1e-3), math.log(1e-1)))
    dt_bias = dt0 + jnp.log(-jnp.expm1(-dt0))
    a_log = jnp.log(jax.random.uniform(ks[6], (DEPTH, SSD_HEADS), f32, 1.0, 16.0))
    d_skip = 1.0 + nrm(ks[7], (DEPTH, SSD_HEADS), 0.1)
    ssd_norm = 1.0 + nrm(ks[8], (DEPTH, SSD_WIDTH), 0.02)
    q_norm = 1.0 + nrm(ks[9], (DEPTH, ATT_HEAD_DIM), 0.02)
    k_norm = 1.0 + nrm(ks[10], (DEPTH, ATT_HEAD_DIM), 0.02)
    ret_norm = 1.0 + nrm(ks[11], (DEPTH, RET_V_WIDTH), 0.02)
    w_out = nrm(ks[12], (DEPTH, MIX_WIDTH, D_MODEL), MIX_WIDTH ** -0.5)
    ln_ffn = 1.0 + nrm(ks[13], (DEPTH, D_MODEL), 0.02)
    w_gate = nrm(ks[14], (DEPTH, D_MODEL, D_FF), D_MODEL ** -0.5)
    w_up = nrm(ks[15], (DEPTH, D_MODEL, D_FF), D_MODEL ** -0.5)
    w_down = nrm(ks[16], (DEPTH, D_FF, D_MODEL), D_FF ** -0.5)
    return {'x': x, 'ln_mix': ln_mix, 'w_in': w_in, 'conv_w': conv_w, 'conv_b': conv_b,
            'dt_bias': dt_bias, 'a_log': a_log, 'd_skip': d_skip, 'ssd_norm': ssd_norm,
            'q_norm': q_norm, 'k_norm': k_norm, 'ret_norm': ret_norm, 'w_out': w_out,
            'ln_ffn': ln_ffn, 'w_gate': w_gate, 'w_up': w_up, 'w_down': w_down}


def reference(x, ln_mix, w_in, conv_w, conv_b, dt_bias, a_log, d_skip, ssd_norm,
              q_norm, k_norm, ret_norm, w_out, ln_ffn, w_gate, w_up, w_down):
    cos, sin = rope_tables(x.shape[1], ATT_HEAD_DIM)
    for i in range(DEPTH):
        x = x + hybrid_mixer(rms_norm(x, ln_mix[i]), w_in[i], conv_w[i], conv_b[i],
                             dt_bias[i], a_log[i], d_skip[i], ssd_norm[i],
                             q_norm[i], k_norm[i], ret_norm[i], w_out[i], cos, sin)
        x = x + swiglu(rms_norm(x, ln_ffn[i]), w_gate[i], w_up[i], w_down[i])
    return x
```

```python
import functools
import math

import numpy as np
import jax
import jax.numpy as jnp
from jax import lax
from jax.experimental import pallas as pl
from jax.experimental.pallas import tpu as pltpu

F32 = jnp.float32
BF16 = jnp.bfloat16
NEG_INF = float("-inf")

D_MODEL = 2048
EPS = 1e-6
ROPE_THETA = 10000.0

SSD_HEADS = 16
SSD_HEAD_DIM = 64
SSD_WIDTH = SSD_HEADS * SSD_HEAD_DIM
SSD_GROUPS = 2
SSD_STATE = 128
SSD_CONV = 4
SSD_CONV_CH = SSD_WIDTH + 2 * SSD_GROUPS * SSD_STATE
CHUNK = 128

ATT_HEADS = 8
HEAD_DIM = 64
ATT_WIDTH = ATT_HEADS * HEAD_DIM
DILATED_PAIRS = ((128, 1), (512, 4), (2048, 16))
ATT_BLOCK = 128

RET_HEADS = 4
RET_QK_WIDTH = RET_HEADS * 64
RET_V_DIM = 128
RET_V_WIDTH = RET_HEADS * RET_V_DIM

MIX_WIDTH = SSD_WIDTH + ATT_WIDTH + RET_V_WIDTH
D_FF = 5632
DT_COL = SSD_WIDTH + SSD_CONV_CH
IN_MAIN = DT_COL + 3 * ATT_WIDTH + 2 * RET_QK_WIDTH + 2 * RET_V_WIDTH

LANE = 128
SUBLANE = 8
PB = 512
BLK_BC = (2 * SSD_WIDTH) // PB
BLK_AQ, BLK_AK, BLK_AV = 5, 6, 7
BLK_RQK, BLK_RV, BLK_RG = 8, 9, 10
PROJ_BLKS = IN_MAIN // PB
QKR_WIDTH = 3 * PB
VMEM_LIMIT = 56 * 1024 * 1024


def _cparams(sem):
    return pltpu.CompilerParams(dimension_semantics=sem, vmem_limit_bytes=VMEM_LIMIT)


def _sds(shape, dtype):
    return jax.ShapeDtypeStruct(shape, dtype)


def _nt_dot(a, b):
    return lax.dot_general(a, b, (((1,), (1,)), ((), ())), preferred_element_type=F32)


def _tn_dot(a, b):
    return lax.dot_general(a, b, (((0,), (0,)), ((), ())), preferred_element_type=F32)


def _dot(a, b):
    return jnp.dot(a, b, preferred_element_type=F32)


def _split3_dot(lhs_bf16, x):
    hi = x.astype(BF16)
    r = x - hi.astype(F32)
    mid = r.astype(BF16)
    lo = (r - mid.astype(F32)).astype(BF16)
    return _dot(lhs_bf16, hi) + _dot(lhs_bf16, mid) + _dot(lhs_bf16, lo)


def _inproj_kernel(x_ref, g_ref, w_ref, wdt_ref, proj_ref, dt_ref, xn_ref):
    @pl.when(pl.program_id(1) == 0)
    def _():
        x = x_ref[...]
        ms = jnp.mean(x * x, axis=-1, keepdims=True)
        xn_ref[...] = ((x * lax.rsqrt(ms + EPS)) * g_ref[...]).astype(BF16)
        dt_ref[...] = _dot(xn_ref[...], wdt_ref[...])

    proj_ref[...] = _dot(xn_ref[...], w_ref[...]).astype(BF16)


def _inproj(x2d, gain, w_main, w_dt, tm, tn):
    t = x2d.shape[0]
    return pl.pallas_call(
        _inproj_kernel,
        grid=(t // tm, IN_MAIN // tn),
        in_specs=[
            pl.BlockSpec((tm, D_MODEL), lambda i, j: (i, 0)),
            pl.BlockSpec((1, D_MODEL), lambda i, j: (0, 0)),
            pl.BlockSpec((D_MODEL, tn), lambda i, j: (0, j)),
            pl.BlockSpec((D_MODEL, LANE), lambda i, j: (0, 0)),
        ],
        out_specs=[
            pl.BlockSpec((tm, tn), lambda i, j: (i, j)),
            pl.BlockSpec((tm, LANE), lambda i, j: (i, 0)),
        ],
        out_shape=[_sds((t, IN_MAIN), BF16), _sds((t, LANE), F32)],
        scratch_shapes=[pltpu.VMEM((tm, D_MODEL), BF16)],
        compiler_params=_cparams(("parallel", "arbitrary")),
        name="inproj",
    )(x2d, gain, w_main, w_dt)


def _prep_kernel(aq_ref, ak_ref, rqk_ref, cos_ref, sin_ref, qg_ref, kg_ref, gm_ref, out_ref):
    cosf = cos_ref[...]
    sinf = sin_ref[...]
    lane = lax.broadcasted_iota(jnp.int32, cosf.shape, 1)
    first_half = (lane & (HEAD_DIM // 2)) == 0

    def rope(xc):
        swapped = jnp.where(first_half, pltpu.roll(xc, LANE - HEAD_DIM // 2, axis=1),
                            pltpu.roll(xc, HEAD_DIM // 2, axis=1))
        return xc * cosf + swapped * sinf

    def head_rms(x_ref, gain_ref):
        x = x_ref[...].astype(F32)
        sq = x * x
        hi = sq.astype(BF16)
        lo = (sq - hi.astype(F32)).astype(BF16)
        ss = _dot(hi, gm_ref[...]) + _dot(lo, gm_ref[...])
        return (x * lax.rsqrt(ss * (1.0 / HEAD_DIM) + EPS)) * gain_ref[...]

    q = head_rms(aq_ref, qg_ref)
    k = head_rms(ak_ref, kg_ref)
    r = rqk_ref[...].astype(F32)
    scale = HEAD_DIM ** -0.5
    for c in range(PB // LANE):
        cs = slice(c * LANE, (c + 1) * LANE)
        out_ref[:, cs] = (rope(q[:, cs]) * scale).astype(BF16)
        out_ref[:, PB + c * LANE:PB + (c + 1) * LANE] = rope(k[:, cs]).astype(BF16)
        rr = rope(r[:, cs])
        if c * LANE >= RET_QK_WIDTH:
            rr = rr * scale
        out_ref[:, 2 * PB + c * LANE:2 * PB + (c + 1) * LANE] = rr.astype(BF16)


def _prep(proj, cosf, sinf, qg, kg, gmat, seq, tm):
    t = proj.shape[0]
    nseq = seq // tm
    return pl.pallas_call(
        _prep_kernel,
        grid=(t // tm,),
        in_specs=[
            pl.BlockSpec((tm, PB), lambda i: (i, BLK_AQ)),
            pl.BlockSpec((tm, PB), lambda i: (i, BLK_AK)),
            pl.BlockSpec((tm, PB), lambda i: (i, BLK_RQK)),
            pl.BlockSpec((tm, LANE), lambda i: (i % nseq, 0)),
            pl.BlockSpec((tm, LANE), lambda i: (i % nseq, 0)),
            pl.BlockSpec((1, PB), lambda i: (0, 0)),
            pl.BlockSpec((1, PB), lambda i: (0, 0)),
            pl.BlockSpec((PB, PB), lambda i: (0, 0)),
        ],
        out_specs=pl.BlockSpec((tm, QKR_WIDTH), lambda i: (i, 0)),
        out_shape=_sds((t, QKR_WIDTH), BF16),
        compiler_params=_cparams(("parallel",)),
        name="qkprep",
    )(proj, proj, proj, cosf, sinf, qg, kg, gmat)


def _expand_heads(q, lo_half):
    rows = q.shape[0]
    tiles = []
    for jp in range(SSD_HEADS // 2):
        c0 = jnp.broadcast_to(q[:, 2 * jp:2 * jp + 1], (rows, LANE))
        c1 = jnp.broadcast_to(q[:, 2 * jp + 1:2 * jp + 2], (rows, LANE))
        tiles.append(jnp.where(lo_half, c0, c1))
    return jnp.concatenate(tiles, axis=1)


def _ssd_kernel(z_ref, xs_ref, bc_ref, dt_ref, cw_ref, cb_ref, dtb_ref, alog_ref, dsk_ref, gn_ref,
                y_ref, xpad_ref, st_ref, *, ts):
    @pl.when(pl.program_id(1) == 0)
    def _():
        xpad_ref[0:SUBLANE, :] = jnp.zeros((SUBLANE, SSD_CONV_CH), F32)
        st_ref[...] = jnp.zeros(st_ref.shape, F32)

    xpad_ref[SUBLANE:SUBLANE + ts, 0:SSD_WIDTH] = xs_ref[...].astype(F32)
    xpad_ref[SUBLANE:SUBLANE + ts, SSD_WIDTH:SSD_CONV_CH] = bc_ref[...].astype(F32)

    row = lax.broadcasted_iota(jnp.int32, (CHUNK, CHUNK), 0)
    col = lax.broadcasted_iota(jnp.int32, (CHUNK, CHUNK), 1)
    causal = row >= col
    ltri = jnp.where(causal, 1.0, 0.0).astype(BF16)
    lo_half = col < HEAD_DIM
    a_neg = -jnp.exp(alog_ref[...])
    gw = SSD_WIDTH // SSD_GROUPS

    for c in range(ts // CHUNK):
        r0 = c * CHUNK
        acc = cb_ref[...]
        for i in range(SSD_CONV):
            off = r0 + SUBLANE - (SSD_CONV - 1) + i
            acc = acc + cw_ref[i:i + 1, :] * xpad_ref[off:off + CHUNK, :]
        xc = jax.nn.silu(acc)
        xs = xc[:, 0:SSD_WIDTH]
        bmat = xc[:, SSD_WIDTH:SSD_WIDTH + SSD_GROUPS * SSD_STATE].astype(BF16)
        cmat = xc[:, SSD_WIDTH + SSD_GROUPS * SSD_STATE:SSD_CONV_CH].astype(BF16)

        dt = jax.nn.softplus(dt_ref[r0:r0 + CHUNK, :] + dtb_ref[...])
        acum = _split3_dot(ltri, dt * a_neg)
        acum_t = acum.T
        dt_t = dt.T
        a_last = acum[CHUNK - 1:CHUNK, :]
        w_end = _expand_heads(dt * jnp.exp(a_last - acum), lo_half)
        ea = _expand_heads(jnp.exp(acum), lo_half)
        xw = (xs * w_end).astype(BF16)
        xs_b = xs.astype(BF16)

        y_tiles = []
        for g in range(SSD_GROUPS):
            bg = bmat[:, g * SSD_STATE:(g + 1) * SSD_STATE]
            cg = cmat[:, g * SSD_STATE:(g + 1) * SSD_STATE]
            cbm = _nt_dot(cg, bg)
            st_prev = st_ref[:, g * gw:(g + 1) * gw]
            y_off = _dot(cg, st_prev.astype(BF16)) * ea[:, g * gw:(g + 1) * gw]
            chunk_state = _tn_dot(bg, xw[:, g * gw:(g + 1) * gw])
            st_ref[:, g * gw:(g + 1) * gw] = st_prev * ea[CHUNK - 1:CHUNK, g * gw:(g + 1) * gw] + chunk_state
            for hp in range(SSD_HEADS // SSD_GROUPS // 2):
                h0 = g * (SSD_HEADS // SSD_GROUPS) + 2 * hp
                mats = []
                for h in (h0, h0 + 1):
                    seg = jnp.broadcast_to(acum[:, h:h + 1], (CHUNK, CHUNK)) - acum_t[h:h + 1, :]
                    dec = jnp.exp(jnp.where(causal, seg, NEG_INF))
                    mats.append((cbm * dec * dt_t[h:h + 1, :]).astype(BF16))
                yd = _dot(jnp.concatenate(mats, axis=0), xs_b[:, h0 * HEAD_DIM:h0 * HEAD_DIM + LANE])
                lt = slice(hp * LANE, (hp + 1) * LANE)
                y_tiles.append(jnp.where(lo_half, yd[0:CHUNK], yd[CHUNK:2 * CHUNK]) + y_off[:, lt])
        y = jnp.concatenate(y_tiles, axis=1) + xs * dsk_ref[...]
        y = y * jax.nn.silu(z_ref[r0:r0 + CHUNK, :].astype(F32))
        for g in range(SSD_GROUPS):
            yg = y[:, g * gw:(g + 1) * gw]
            ms = jnp.mean(yg * yg, axis=-1, keepdims=True)
            y_ref[r0:r0 + CHUNK, g * gw:(g + 1) * gw] = (
                yg * lax.rsqrt(ms + EPS) * gn_ref[:, g * gw:(g + 1) * gw]).astype(BF16)

    xpad_ref[0:SUBLANE, :] = xpad_ref[ts:ts + SUBLANE, :]


def _ssd(proj, dt_raw, conv_w, conv_b, dt_bias, a_log, d_skip, gain, batch, seq, ts):
    t = proj.shape[0]
    ns = seq // ts
    full = lambda shape: pl.BlockSpec(shape, lambda b, s: (0, 0))
    return pl.pallas_call(
        functools.partial(_ssd_kernel, ts=ts),
        grid=(batch, ns),
        in_specs=[
            pl.BlockSpec((ts, SSD_WIDTH), lambda b, s: (b * ns + s, 0)),
            pl.BlockSpec((ts, SSD_WIDTH), lambda b, s: (b * ns + s, 1)),
            pl.BlockSpec((ts, PB), lambda b, s: (b * ns + s, BLK_BC)),
            pl.BlockSpec((ts, LANE), lambda b, s: (b * ns + s, 0)),
            full((SSD_CONV, SSD_CONV_CH)), full((1, SSD_CONV_CH)), full((1, LANE)), full((1, LANE)),
            full((1, SSD_WIDTH)), full((1, SSD_WIDTH)),
        ],
        out_specs=pl.BlockSpec((ts, SSD_WIDTH), lambda b, s: (b * ns + s, 0)),
        out_shape=_sds((t, SSD_WIDTH), BF16),
        scratch_shapes=[pltpu.VMEM((ts + SUBLANE, SSD_CONV_CH), F32), pltpu.VMEM((SSD_STATE, SSD_WIDTH), F32)],
        compiler_params=_cparams(("parallel", "arbitrary")),
        name="ssd",
    )(proj, proj, proj, dt_raw, conv_w, conv_b, dt_bias, a_log, d_skip, gain)


def _att_kernel(q_ref, kp_ref, kc_ref, vp_ref, vc_ref, o_ref, lse_ref, *, tq):
    n = pl.program_id(2)
    blk = ATT_BLOCK
    iq = lax.broadcasted_iota(jnp.int32, (2 * blk, 2 * blk), 0) & (blk - 1)
    ik = lax.broadcasted_iota(jnp.int32, (2 * blk, 2 * blk), 1)
    delta = ik - iq
    band = (delta >= 0) & (delta <= blk)
    bias = jnp.where(band, 0.0, NEG_INF)
    before_start = jnp.where(n > 0, 0.0, NEG_INF)
    bias_first = jnp.where(ik >= blk, bias, jnp.minimum(bias, before_start))
    lo_half = lax.broadcasted_iota(jnp.int32, (blk, LANE), 1) < HEAD_DIM
    ones_kv = jnp.ones((2 * blk, LANE), BF16)

    for i in range(tq // blk):
        rs = slice(i * blk, (i + 1) * blk)
        for j in range(ATT_WIDTH // LANE):
            cs = slice(j * LANE, (j + 1) * LANE)
            q2 = q_ref[rs, cs]
            zero = jnp.zeros_like(q2)
            qq = jnp.concatenate([jnp.where(lo_half, q2, zero), jnp.where(lo_half, zero, q2)], axis=0)
            if i == 0:
                k2 = jnp.concatenate([kp_ref[:, cs], kc_ref[0:blk, cs]], axis=0)
                v2 = jnp.concatenate([vp_ref[:, cs], vc_ref[0:blk, cs]], axis=0)
            else:
                k2 = kc_ref[(i - 1) * blk:(i + 1) * blk, cs]
                v2 = vc_ref[(i - 1) * blk:(i + 1) * blk, cs]
            s = _nt_dot(qq, k2) + (bias_first if i == 0 else bias)
            m = jnp.max(s, axis=-1, keepdims=True)
            p = jnp.exp(s - m).astype(BF16)
            pv = _dot(p, v2)
            den = _dot(p, ones_kv)
            out = pv / den
            lse = m + jnp.log(den)
            o_ref[rs, cs] = jnp.where(lo_half, out[0:blk], out[blk:2 * blk]).astype(BF16)
            lse_ref[rs, cs] = jnp.where(lo_half, lse[0:blk], lse[blk:2 * blk])


def _att_branch(qkr, proj, batch, seq, dil, tq):
    t = qkr.shape[0]
    sub = seq // dil
    nq = sub // tq
    kb = tq // ATT_BLOCK
    qkr_v = qkr.reshape(t // dil, dil * QKR_WIDTH)
    proj_v = proj.reshape(t // dil, dil * IN_MAIN)
    qkr_blks = QKR_WIDTH // PB

    def prev_rows(b, r, n):
        return (b * (sub // ATT_BLOCK) + jnp.maximum(n * kb - 1, 0))

    out, lse = pl.pallas_call(
        functools.partial(_att_kernel, tq=tq),
        grid=(batch, dil, nq),
        in_specs=[
            pl.BlockSpec((tq, PB), lambda b, r, n: (b * nq + n, r * qkr_blks)),
            pl.BlockSpec((ATT_BLOCK, PB), lambda b, r, n: (prev_rows(b, r, n), r * qkr_blks + 1)),
            pl.BlockSpec((tq, PB), lambda b, r, n: (b * nq + n, r * qkr_blks + 1)),
            pl.BlockSpec((ATT_BLOCK, PB), lambda b, r, n: (prev_rows(b, r, n), r * PROJ_BLKS + BLK_AV)),
            pl.BlockSpec((tq, PB), lambda b, r, n: (b * nq + n, r * PROJ_BLKS + BLK_AV)),
        ],
        out_specs=[
            pl.BlockSpec((tq, PB), lambda b, r, n: (b * nq + n, r)),
            pl.BlockSpec((tq, PB), lambda b, r, n: (b * nq + n, r)),
        ],
        out_shape=[_sds((t // dil, dil * ATT_WIDTH), BF16), _sds((t // dil, dil * ATT_WIDTH), F32)],
        compiler_params=_cparams(("parallel", "parallel", "arbitrary")),
        name=f"att_d{dil}",
    )(qkr_v, qkr_v, qkr_v, proj_v, proj_v)
    return out.reshape(t, ATT_WIDTH), lse.reshape(t, ATT_WIDTH)


def _ret_kernel(q_ref, k_ref, v_ref, g_ref, gn_ref, y_ref, st_ref, *, ts):
    @pl.when(pl.program_id(1) == 0)
    def _():
        st_ref[...] = jnp.zeros(st_ref.shape, F32)

    row_i = lax.broadcasted_iota(jnp.int32, (CHUNK, LANE), 0)
    col_i = lax.broadcasted_iota(jnp.int32, (CHUNK, LANE), 1)
    row = row_i.astype(F32)
    rel = (row_i - col_i).astype(F32)
    lo_half = col_i < 64
    log_gamma = [math.log1p(-(2.0 ** (-5.0 - h))) for h in range(RET_HEADS)]

    for c in range(ts // CHUNK):
        rs = slice(c * CHUNK, (c + 1) * CHUNK)
        for jp in range(RET_HEADS // 2):
            cs = slice(jp * LANE, (jp + 1) * LANE)
            lg_a, lg_b = log_gamma[2 * jp], log_gamma[2 * jp + 1]
            lg_lane = jnp.where(lo_half, lg_a, lg_b)
            q2 = q_ref[rs, cs].astype(F32)
            k2 = k_ref[rs, cs].astype(F32)
            zero = jnp.zeros_like(q2)
            qa = jnp.where(lo_half, q2, zero)
            qb = jnp.where(lo_half, zero, q2)
            k2b = k2.astype(BF16)
            scores = _nt_dot(jnp.concatenate([qa, qb], axis=0).astype(BF16), k2b)
            k_end = (k2 * jnp.exp((CHUNK - 1.0 - row) * lg_lane)).astype(BF16)
            q_start = jnp.exp((row + 1.0) * lg_lane)
            st_pair = st_ref[jp * LANE:(jp + 1) * LANE, :]
            st_b = st_pair.astype(BF16)
            new_rows = []
            for hh, (lg, qh) in enumerate(((lg_a, qa), (lg_b, qb))):
                h = 2 * jp + hh
                vh = v_ref[rs, h * RET_V_DIM:(h + 1) * RET_V_DIM]
                decay = jnp.where(rel >= 0, jnp.exp(jnp.maximum(rel, 0.0) * lg), 0.0)
                inner = _dot((scores[hh * CHUNK:(hh + 1) * CHUNK] * decay).astype(BF16), vh)
                cross = _dot((qh * q_start).astype(BF16), st_b)
                kv = _tn_dot(k_end, vh)
                half = slice(hh * 64, (hh + 1) * 64)
                new_rows.append(st_pair[half] * math.exp(CHUNK * lg) + kv[half])
                yh = inner + cross
                ms = jnp.mean(yh * yh, axis=-1, keepdims=True)
                hs = slice(h * RET_V_DIM, (h + 1) * RET_V_DIM)
                y_ref[rs, hs] = (yh * lax.rsqrt(ms + EPS) * gn_ref[:, hs]
                                 * jax.nn.silu(g_ref[rs, hs].astype(F32))).astype(BF16)
            st_ref[jp * LANE:(jp + 1) * LANE, :] = jnp.concatenate(new_rows, axis=0)


def _ret(qkr, proj, gain, batch, seq, ts):
    t = qkr.shape[0]
    ns = seq // ts
    half = RET_QK_WIDTH
    return pl.pallas_call(
        functools.partial(_ret_kernel, ts=ts),
        grid=(batch, ns),
        in_specs=[
            pl.BlockSpec((ts, half), lambda b, s: (b * ns + s, 2 * PB // half)),
            pl.BlockSpec((ts, half), lambda b, s: (b * ns + s, 2 * PB // half + 1)),
            pl.BlockSpec((ts, PB), lambda b, s: (b * ns + s, BLK_RV)),
            pl.BlockSpec((ts, PB), lambda b, s: (b * ns + s, BLK_RG)),
            pl.BlockSpec((1, RET_V_WIDTH), lambda b, s: (0, 0)),
        ],
        out_specs=pl.BlockSpec((ts, RET_V_WIDTH), lambda b, s: (b * ns + s, 0)),
        out_shape=_sds((t, RET_V_WIDTH), BF16),
        scratch_shapes=[pltpu.VMEM((RET_QK_WIDTH, RET_V_DIM), F32)],
        compiler_params=_cparams(("parallel", "arbitrary")),
        name="retention",
    )(qkr, qkr, proj, proj, gain)


def _outproj_kernel(ys_ref, o1_ref, o2_ref, o3_ref, l1_ref, l2_ref, l3_ref, yr_ref, x_ref, w_ref, g_ref,
                    xo_ref, hn_ref):
    l1, l2, l3 = l1_ref[...], l2_ref[...], l3_ref[...]
    m = jnp.maximum(jnp.maximum(l1, l2), l3)
    e1, e2, e3 = jnp.exp(l1 - m), jnp.exp(l2 - m), jnp.exp(l3 - m)
    y_att = (e1 * o1_ref[...].astype(F32) + e2 * o2_ref[...].astype(F32) + e3 * o3_ref[...].astype(F32)) / (
        e1 + e2 + e3)
    acc = x_ref[...]
    acc = acc + _dot(ys_ref[...], w_ref[0:SSD_WIDTH, :])
    acc = acc + _dot(y_att.astype(BF16), w_ref[SSD_WIDTH:SSD_WIDTH + ATT_WIDTH, :])
    acc = acc + _dot(yr_ref[...], w_ref[SSD_WIDTH + ATT_WIDTH:MIX_WIDTH, :])
    xo_ref[...] = acc
    ms = jnp.mean(acc * acc, axis=-1, keepdims=True)
    hn_ref[...] = ((acc * lax.rsqrt(ms + EPS)) * g_ref[...]).astype(BF16)


def _outproj(y_ssd, outs, lses, y_ret, x2d, w_out, gain, tm):
    t = x2d.shape[0]
    rows = lambda width: pl.BlockSpec((tm, width), lambda i: (i, 0))
    return pl.pallas_call(
        _outproj_kernel,
        grid=(t // tm,),
        in_specs=[rows(SSD_WIDTH)] + [rows(ATT_WIDTH)] * 6 + [rows(RET_V_WIDTH), rows(D_MODEL),
                  pl.BlockSpec((MIX_WIDTH, D_MODEL), lambda i: (0, 0)), pl.BlockSpec((1, D_MODEL), lambda i: (0, 0))],
        out_specs=[rows(D_MODEL), rows(D_MODEL)],
        out_shape=[_sds((t, D_MODEL), F32), _sds((t, D_MODEL), BF16)],
        compiler_params=_cparams(("parallel",)),
        name="outproj",
    )(y_ssd, *outs, *lses, y_ret, x2d, w_out, gain)


def _ffn_kernel(hn_ref, x_ref, wg_ref, wu_ref, wd_ref, o_ref):
    @pl.when(pl.program_id(1) == 0)
    def _():
        o_ref[...] = x_ref[...]

    h = hn_ref[...]
    act = (jax.nn.silu(_dot(h, wg_ref[...])) * _dot(h, wu_ref[...])).astype(BF16)
    o_ref[...] += _dot(act, wd_ref[...])


def _ffn(hn, x2d, w_gate, w_up, w_down, tm, tf):
    t = x2d.shape[0]
    return pl.pallas_call(
        _ffn_kernel,
        grid=(t // tm, D_FF // tf),
        in_specs=[
            pl.BlockSpec((tm, D_MODEL), lambda i, j: (i, 0)),
            pl.BlockSpec((tm, D_MODEL), lambda i, j: (i, 0)),
            pl.BlockSpec((D_MODEL, tf), lambda i, j: (0, j)),
            pl.BlockSpec((D_MODEL, tf), lambda i, j: (0, j)),
            pl.BlockSpec((tf, D_MODEL), lambda i, j: (j, 0)),
        ],
        out_specs=pl.BlockSpec((tm, D_MODEL), lambda i, j: (i, 0)),
        out_shape=_sds((t, D_MODEL), F32),
        compiler_params=_cparams(("parallel", "arbitrary")),
        name="ffn",
    )(hn, x2d, w_gate, w_up, w_down)


def _tiles(seq):
    return dict(
        inproj=(min(512, seq), 512),
        prep=min(512, seq),
        scan=min(256, seq),
        outproj=min(256, seq),
        ffn=(min(512, seq), 512),
    )


def _rope_tables(seq):
    pos = jnp.arange(seq, dtype=F32)
    inv = ROPE_THETA ** (-jnp.arange(0, HEAD_DIM, 2, dtype=F32) / HEAD_DIM)
    ang = pos[:, None] * inv[None, :]
    cos, sin = jnp.cos(ang), jnp.sin(ang)
    cosf = jnp.concatenate([cos, cos, cos, cos], axis=-1)
    sinf = jnp.concatenate([-sin, sin, -sin, sin], axis=-1)
    return cosf, sinf


def _head_sum_matrix():
    idx = np.arange(PB) // HEAD_DIM
    return jnp.asarray(idx[:, None] == idx[None, :], dtype=BF16)


def kernel(x, ln_mix, w_in, conv_w, conv_b, dt_bias, a_log, d_skip, ssd_norm, q_norm, k_norm, ret_norm, w_out,
           ln_ffn, w_gate, w_up, w_down):
    batch, seq, _ = x.shape
    depth = w_in.shape[0]
    t = batch * seq
    tiles = _tiles(seq)
    cosf, sinf = _rope_tables(seq)
    gmat = _head_sum_matrix()
    lane_pad = lambda v: jnp.pad(v.astype(F32), (0, LANE - v.shape[0]))[None, :]

    x2d = x.reshape(t, D_MODEL)
    for i in range(depth):
        w_i = w_in[i]
        w_main = jnp.concatenate([w_i[:, :DT_COL], w_i[:, DT_COL + SSD_HEADS:]], axis=1).astype(BF16)
        w_dt = jnp.pad(w_i[:, DT_COL:DT_COL + SSD_HEADS], ((0, 0), (0, LANE - SSD_HEADS))).astype(BF16)
        proj, dt_raw = _inproj(x2d, ln_mix[i][None, :], w_main, w_dt, *tiles["inproj"])

        qkr = _prep(proj, cosf, sinf, jnp.tile(q_norm[i], ATT_HEADS)[None, :], jnp.tile(k_norm[i], ATT_HEADS)[None, :],
                    gmat, seq, tiles["prep"])
        y_ssd = _ssd(proj, dt_raw, conv_w[i], conv_b[i][None, :], lane_pad(dt_bias[i]), lane_pad(a_log[i]),
                     jnp.repeat(d_skip[i], SSD_HEAD_DIM)[None, :], ssd_norm[i][None, :], batch, seq, tiles["scan"])
        outs, lses = [], []
        for window, dil in DILATED_PAIRS:
            assert window // dil == ATT_BLOCK
            o, l = _att_branch(qkr, proj, batch, seq, dil, min(512, seq // dil))
            outs.append(o)
            lses.append(l)
        y_ret = _ret(qkr, proj, ret_norm[i][None, :], batch, seq, tiles["scan"])

        x2d, hn = _outproj(y_ssd, outs, lses, y_ret, x2d, w_out[i].astype(BF16), ln_ffn[i][None, :], tiles["outproj"])
        x2d = _ffn(hn, x2d, w_gate[i].astype(BF16), w_up[i].astype(BF16), w_down[i].astype(BF16), *tiles["ffn"])
    return x2d.reshape(batch, seq, D_MODEL)
```

```python
import functools
import math

import numpy as np
import jax
import jax.numpy as jnp
from jax import lax
from jax.experimental import pallas as pl
from jax.experimental.pallas import tpu as pltpu

F32 = jnp.float32
BF16 = jnp.bfloat16
NEG_INF = float("-inf")

D_MODEL = 2048
EPS = 1e-6
ROPE_THETA = 10000.0

SSD_HEADS = 16
SSD_HEAD_DIM = 64
SSD_WIDTH = SSD_HEADS * SSD_HEAD_DIM
SSD_GROUPS = 2
SSD_STATE = 128
SSD_CONV = 4
SSD_CONV_CH = SSD_WIDTH + 2 * SSD_GROUPS * SSD_STATE
CHUNK = 128

ATT_HEADS = 8
HEAD_DIM = 64
ATT_WIDTH = ATT_HEADS * HEAD_DIM
DILATED_PAIRS = ((128, 1), (512, 4), (2048, 16))
ATT_BLOCK = 128

RET_HEADS = 4
RET_QK_WIDTH = RET_HEADS * 64
RET_V_DIM = 128
RET_V_WIDTH = RET_HEADS * RET_V_DIM

MIX_WIDTH = SSD_WIDTH + ATT_WIDTH + RET_V_WIDTH
D_FF = 5632
DT_COL = SSD_WIDTH + SSD_CONV_CH
IN_MAIN = DT_COL + 3 * ATT_WIDTH + 2 * RET_QK_WIDTH + 2 * RET_V_WIDTH

LANE = 128
SUBLANE = 8
PB = 512
BLK_BC = (2 * SSD_WIDTH) // PB
BLK_AQ, BLK_AK, BLK_AV = 5, 6, 7
BLK_RQK, BLK_RV, BLK_RG = 8, 9, 10
PROJ_BLKS = IN_MAIN // PB
QKR_WIDTH = 3 * PB
VMEM_LIMIT = 56 * 1024 * 1024


def _cparams(sem):
    return pltpu.CompilerParams(dimension_semantics=sem, vmem_limit_bytes=VMEM_LIMIT)


def _sds(shape, dtype):
    return jax.ShapeDtypeStruct(shape, dtype)


def _nt_dot(a, b):
    return lax.dot_general(a, b, (((1,), (1,)), ((), ())), preferred_element_type=F32)


def _tn_dot(a, b):
    return lax.dot_general(a, b, (((0,), (0,)), ((), ())), preferred_element_type=F32)


def _dot(a, b):
    return jnp.dot(a, b, preferred_element_type=F32)


def _split3_dot(lhs_bf16, x):
    hi = x.astype(BF16)
    r = x - hi.astype(F32)
    mid = r.astype(BF16)
    lo = (r - mid.astype(F32)).astype(BF16)
    return _dot(lhs_bf16, hi) + _dot(lhs_bf16, mid) + _dot(lhs_bf16, lo)


def _inproj_kernel(x_ref, g_ref, w_ref, wdt_ref, proj_ref, dt_ref, xn_ref):
    @pl.when(pl.program_id(1) == 0)
    def _():
        x = x_ref[...]
        ms = jnp.mean(x * x, axis=-1, keepdims=True)
        xn_ref[...] = ((x * lax.rsqrt(ms + EPS)) * g_ref[...]).astype(BF16)
        dt_ref[...] = _dot(xn_ref[...], wdt_ref[...])

    proj_ref[...] = _dot(xn_ref[...], w_ref[...]).astype(BF16)


def _inproj(x2d, gain, w_main, w_dt, tm, tn):
    t = x2d.shape[0]
    return pl.pallas_call(
        _inproj_kernel,
        grid=(t // tm, IN_MAIN // tn),
        in_specs=[
            pl.BlockSpec((tm, D_MODEL), lambda i, j: (i, 0)),
            pl.BlockSpec((1, D_MODEL), lambda i, j: (0, 0)),
            pl.BlockSpec((D_MODEL, tn), lambda i, j: (0, j)),
            pl.BlockSpec((D_MODEL, LANE), lambda i, j: (0, 0)),
        ],
        out_specs=[
            pl.BlockSpec((tm, tn), lambda i, j: (i, j)),
            pl.BlockSpec((tm, LANE), lambda i, j: (i, 0)),
        ],
        out_shape=[_sds((t, IN_MAIN), BF16), _sds((t, LANE), F32)],
        scratch_shapes=[pltpu.VMEM((tm, D_MODEL), BF16)],
        compiler_params=_cparams(("parallel", "arbitrary")),
        name="inproj",
    )(x2d, gain, w_main, w_dt)


def _prep_kernel(aq_ref, ak_ref, av_ref, rqk_ref, cos_ref, sin_ref, qg_ref, kg_ref, gm_ref,
                 tok_ref, *rest, tm, dils):
    dil_refs, slab_ref = rest[:-1], rest[-1]
    cosf = cos_ref[...]
    sinf = sin_ref[...]
    lane = lax.broadcasted_iota(jnp.int32, cosf.shape, 1)
    first_half = (lane & (HEAD_DIM // 2)) == 0

    def rope(xc):
        swapped = jnp.where(first_half, pltpu.roll(xc, LANE - HEAD_DIM // 2, axis=1),
                            pltpu.roll(xc, HEAD_DIM // 2, axis=1))
        return xc * cosf + swapped * sinf

    def head_rms(x_ref, gain_ref):
        x = x_ref[...].astype(F32)
        sq = x * x
        hi = sq.astype(BF16)
        lo = (sq - hi.astype(F32)).astype(BF16)
        ss = _dot(hi, gm_ref[...]) + _dot(lo, gm_ref[...])
        return (x * lax.rsqrt(ss * (1.0 / HEAD_DIM) + EPS)) * gain_ref[...]

    q = head_rms(aq_ref, qg_ref)
    k = head_rms(ak_ref, kg_ref)
    v = av_ref[...].astype(F32)
    r = rqk_ref[...].astype(F32)
    scale = HEAD_DIM ** -0.5
    n_lt = PB // LANE
    for c in range(n_lt):
        cs = slice(c * LANE, (c + 1) * LANE)
        qc = rope(q[:, cs]) * scale
        kc = rope(k[:, cs])
        rr = rope(r[:, cs])
        if c * LANE >= RET_QK_WIDTH:
            rr = rr * scale
        tok_ref[:, cs] = qc.astype(BF16)
        tok_ref[:, PB + c * LANE:PB + (c + 1) * LANE] = kc.astype(BF16)
        tok_ref[:, 2 * PB + c * LANE:2 * PB + (c + 1) * LANE] = rr.astype(BF16)
        for part, val in enumerate((qc, kc, v[:, cs])):
            slot = part * n_lt + c
            slab_ref[slot] = val
            col = slice(part * PB + c * LANE, part * PB + (c + 1) * LANE)
            for dil, d_ref in zip(dils, dil_refs):
                for res in range(dil):
                    d_ref[res, :, col] = slab_ref[slot, pl.ds(res, tm // dil, stride=dil), :].astype(BF16)


def _prep(proj, cosf, sinf, qg, kg, gmat, batch, seq, tm, dils):
    t = proj.shape[0]
    nseq = seq // tm
    return pl.pallas_call(
        functools.partial(_prep_kernel, tm=tm, dils=dils),
        grid=(t // tm,),
        in_specs=[
            pl.BlockSpec((tm, PB), lambda i: (i, BLK_AQ)),
            pl.BlockSpec((tm, PB), lambda i: (i, BLK_AK)),
            pl.BlockSpec((tm, PB), lambda i: (i, BLK_AV)),
            pl.BlockSpec((tm, PB), lambda i: (i, BLK_RQK)),
            pl.BlockSpec((tm, LANE), lambda i: (i % nseq, 0)),
            pl.BlockSpec((tm, LANE), lambda i: (i % nseq, 0)),
            pl.BlockSpec((1, PB), lambda i: (0, 0)),
            pl.BlockSpec((1, PB), lambda i: (0, 0)),
            pl.BlockSpec((PB, PB), lambda i: (0, 0)),
        ],
        out_specs=[pl.BlockSpec((tm, QKR_WIDTH), lambda i: (i, 0))] + [
            pl.BlockSpec((None, dil, tm // dil, QKR_WIDTH), lambda i: (i // nseq, 0, i % nseq, 0)) for dil in dils],
        out_shape=[_sds((t, QKR_WIDTH), BF16)] + [_sds((batch, dil, seq // dil, QKR_WIDTH), BF16) for dil in dils],
        scratch_shapes=[pltpu.VMEM((3 * (PB // LANE), tm, LANE), F32)],
        compiler_params=_cparams(("parallel",)),
        name="qkprep",
    )(proj, proj, proj, proj, cosf, sinf, qg, kg, gmat)


def _expand_heads(q, lo_half):
    rows = q.shape[0]
    tiles = []
    for jp in range(SSD_HEADS // 2):
        c0 = jnp.broadcast_to(q[:, 2 * jp:2 * jp + 1], (rows, LANE))
        c1 = jnp.broadcast_to(q[:, 2 * jp + 1:2 * jp + 2], (rows, LANE))
        tiles.append(jnp.where(lo_half, c0, c1))
    return jnp.concatenate(tiles, axis=1)


def _ssd_kernel(z_ref, xs_ref, bc_ref, dt_ref, cw_ref, cb_ref, dtb_ref, alog_ref, dsk_ref, gn_ref,
                y_ref, xpad_ref, st_ref, *, ts):
    @pl.when(pl.program_id(1) == 0)
    def _():
        xpad_ref[0:SUBLANE, :] = jnp.zeros((SUBLANE, SSD_CONV_CH), F32)
        st_ref[...] = jnp.zeros(st_ref.shape, F32)

    xpad_ref[SUBLANE:SUBLANE + ts, 0:SSD_WIDTH] = xs_ref[...].astype(F32)
    xpad_ref[SUBLANE:SUBLANE + ts, SSD_WIDTH:SSD_CONV_CH] = bc_ref[...].astype(F32)

    row = lax.broadcasted_iota(jnp.int32, (CHUNK, CHUNK), 0)
    col = lax.broadcasted_iota(jnp.int32, (CHUNK, CHUNK), 1)
    causal = row >= col
    ltri = jnp.where(causal, 1.0, 0.0).astype(BF16)
    lo_half = col < HEAD_DIM
    a_neg = -jnp.exp(alog_ref[...])
    gw = SSD_WIDTH // SSD_GROUPS

    for c in range(ts // CHUNK):
        r0 = c * CHUNK
        acc = cb_ref[...]
        for i in range(SSD_CONV):
            off = r0 + SUBLANE - (SSD_CONV - 1) + i
            acc = acc + cw_ref[i:i + 1, :] * xpad_ref[off:off + CHUNK, :]
        xc = jax.nn.silu(acc)
        xs = xc[:, 0:SSD_WIDTH]
        bmat = xc[:, SSD_WIDTH:SSD_WIDTH + SSD_GROUPS * SSD_STATE].astype(BF16)
        cmat = xc[:, SSD_WIDTH + SSD_GROUPS * SSD_STATE:SSD_CONV_CH].astype(BF16)

        dt = jax.nn.softplus(dt_ref[r0:r0 + CHUNK, :] + dtb_ref[...])
        acum = _split3_dot(ltri, dt * a_neg)
        acum_t = acum.T
        dt_t = dt.T
        a_last = acum[CHUNK - 1:CHUNK, :]
        w_end = _expand_heads(dt * jnp.exp(a_last - acum), lo_half)
        ea = _expand_heads(jnp.exp(acum), lo_half)
        xw = (xs * w_end).astype(BF16)
        xs_b = xs.astype(BF16)

        y_tiles = []
        for g in range(SSD_GROUPS):
            bg = bmat[:, g * SSD_STATE:(g + 1) * SSD_STATE]
            cg = cmat[:, g * SSD_STATE:(g + 1) * SSD_STATE]
            cbm = _nt_dot(cg, bg)
            st_prev = st_ref[:, g * gw:(g + 1) * gw]
            y_off = _dot(cg, st_prev.astype(BF16)) * ea[:, g * gw:(g + 1) * gw]
            chunk_state = _tn_dot(bg, xw[:, g * gw:(g + 1) * gw])
            st_ref[:, g * gw:(g + 1) * gw] = st_prev * ea[CHUNK - 1:CHUNK, g * gw:(g + 1) * gw] + chunk_state
            for hp in range(SSD_HEADS // SSD_GROUPS // 2):
                h0 = g * (SSD_HEADS // SSD_GROUPS) + 2 * hp
                mats = []
                for h in (h0, h0 + 1):
                    seg = jnp.broadcast_to(acum[:, h:h + 1], (CHUNK, CHUNK)) - acum_t[h:h + 1, :]
                    dec = jnp.exp(jnp.where(causal, seg, NEG_INF))
                    mats.append((cbm * dec * dt_t[h:h + 1, :]).astype(BF16))
                yd = _dot(jnp.concatenate(mats, axis=0), xs_b[:, h0 * HEAD_DIM:h0 * HEAD_DIM + LANE])
                lt = slice(hp * LANE, (hp + 1) * LANE)
                y_tiles.append(jnp.where(lo_half, yd[0:CHUNK], yd[CHUNK:2 * CHUNK]) + y_off[:, lt])
        y = jnp.concatenate(y_tiles, axis=1) + xs * dsk_ref[...]
        y = y * jax.nn.silu(z_ref[r0:r0 + CHUNK, :].astype(F32))
        for g in range(SSD_GROUPS):
            yg = y[:, g * gw:(g + 1) * gw]
            ms = jnp.mean(yg * yg, axis=-1, keepdims=True)
            y_ref[r0:r0 + CHUNK, g * gw:(g + 1) * gw] = (
                yg * lax.rsqrt(ms + EPS) * gn_ref[:, g * gw:(g + 1) * gw]).astype(BF16)

    xpad_ref[0:SUBLANE, :] = xpad_ref[ts:ts + SUBLANE, :]


def _ssd(proj, dt_raw, conv_w, conv_b, dt_bias, a_log, d_skip, gain, batch, seq, ts):
    t = proj.shape[0]
    ns = seq // ts
    full = lambda shape: pl.BlockSpec(shape, lambda b, s: (0, 0))
    return pl.pallas_call(
        functools.partial(_ssd_kernel, ts=ts),
        grid=(batch, ns),
        in_specs=[
            pl.BlockSpec((ts, SSD_WIDTH), lambda b, s: (b * ns + s, 0)),
            pl.BlockSpec((ts, SSD_WIDTH), lambda b, s: (b * ns + s, 1)),
            pl.BlockSpec((ts, PB), lambda b, s: (b * ns + s, BLK_BC)),
            pl.BlockSpec((ts, LANE), lambda b, s: (b * ns + s, 0)),
            full((SSD_CONV, SSD_CONV_CH)), full((1, SSD_CONV_CH)), full((1, LANE)), full((1, LANE)),
            full((1, SSD_WIDTH)), full((1, SSD_WIDTH)),
        ],
        out_specs=pl.BlockSpec((ts, SSD_WIDTH), lambda b, s: (b * ns + s, 0)),
        out_shape=_sds((t, SSD_WIDTH), BF16),
        scratch_shapes=[pltpu.VMEM((ts + SUBLANE, SSD_CONV_CH), F32), pltpu.VMEM((SSD_STATE, SSD_WIDTH), F32)],
        compiler_params=_cparams(("parallel", "arbitrary")),
        name="ssd",
    )(proj, proj, proj, dt_raw, conv_w, conv_b, dt_bias, a_log, d_skip, gain)


def _att_kernel(q_ref, kp_ref, kc_ref, vp_ref, vc_ref, o_ref, lse_ref, *, tq):
    n = pl.program_id(2)
    blk = ATT_BLOCK
    iq = lax.broadcasted_iota(jnp.int32, (2 * blk, 2 * blk), 0) & (blk - 1)
    ik = lax.broadcasted_iota(jnp.int32, (2 * blk, 2 * blk), 1)
    delta = ik - iq
    band = (delta >= 0) & (delta <= blk)
    bias = jnp.where(band, 0.0, NEG_INF)
    before_start = jnp.where(n > 0, 0.0, NEG_INF)
    bias_first = jnp.where(ik >= blk, bias, jnp.minimum(bias, before_start))
    lo_half = lax.broadcasted_iota(jnp.int32, (blk, LANE), 1) < HEAD_DIM
    ones_kv = jnp.ones((2 * blk, LANE), BF16)

    for i in range(tq // blk):
        rs = slice(i * blk, (i + 1) * blk)
        for j in range(ATT_WIDTH // LANE):
            cs = slice(j * LANE, (j + 1) * LANE)
            q2 = q_ref[rs, cs]
            zero = jnp.zeros_like(q2)
            qq = jnp.concatenate([jnp.where(lo_half, q2, zero), jnp.where(lo_half, zero, q2)], axis=0)
            if i == 0:
                k2 = jnp.concatenate([kp_ref[:, cs], kc_ref[0:blk, cs]], axis=0)
                v2 = jnp.concatenate([vp_ref[:, cs], vc_ref[0:blk, cs]], axis=0)
            else:
                k2 = kc_ref[(i - 1) * blk:(i + 1) * blk, cs]
                v2 = vc_ref[(i - 1) * blk:(i + 1) * blk, cs]
            s = _nt_dot(qq, k2) + (bias_first if i == 0 else bias)
            m = jnp.max(s, axis=-1, keepdims=True)
            p = jnp.exp(s - m).astype(BF16)
            pv = _dot(p, v2)
            den = _dot(p, ones_kv)
            out = pv / den
            lse = m + jnp.log(den)
            o_ref[rs, cs] = jnp.where(lo_half, out[0:blk], out[blk:2 * blk]).astype(BF16)
            lse_ref[rs, cs] = jnp.where(lo_half, lse[0:blk], lse[blk:2 * blk])


def _att_branch(qk_arr, v_arr, v_blk, tq):
    batch, dil, sub, _ = qk_arr.shape
    kb = tq // ATT_BLOCK
    tile = lambda blk: pl.BlockSpec((None, None, tq, PB), lambda b, r, n: (b, r, n, blk))
    prev = lambda blk: pl.BlockSpec((None, None, ATT_BLOCK, PB),
                                    lambda b, r, n: (b, r, jnp.maximum(n * kb - 1, 0), blk))
    return pl.pallas_call(
        functools.partial(_att_kernel, tq=tq),
        grid=(batch, dil, sub // tq),
        in_specs=[tile(0), prev(1), tile(1), prev(v_blk), tile(v_blk)],
        out_specs=[tile(0), tile(0)],
        out_shape=[_sds((batch, dil, sub, ATT_WIDTH), BF16), _sds((batch, dil, sub, ATT_WIDTH), F32)],
        compiler_params=_cparams(("parallel", "parallel", "arbitrary")),
        name=f"att_d{dil}",
    )(qk_arr, qk_arr, qk_arr, v_arr, v_arr)


def _ret_kernel(q_ref, k_ref, v_ref, g_ref, gn_ref, y_ref, st_ref, *, ts):
    @pl.when(pl.program_id(1) == 0)
    def _():
        st_ref[...] = jnp.zeros(st_ref.shape, F32)

    row_i = lax.broadcasted_iota(jnp.int32, (CHUNK, LANE), 0)
    col_i = lax.broadcasted_iota(jnp.int32, (CHUNK, LANE), 1)
    row = row_i.astype(F32)
    rel = (row_i - col_i).astype(F32)
    lo_half = col_i < 64
    log_gamma = [math.log1p(-(2.0 ** (-5.0 - h))) for h in range(RET_HEADS)]

    for c in range(ts // CHUNK):
        rs = slice(c * CHUNK, (c + 1) * CHUNK)
        for jp in range(RET_HEADS // 2):
            cs = slice(jp * LANE, (jp + 1) * LANE)
            lg_a, lg_b = log_gamma[2 * jp], log_gamma[2 * jp + 1]
            lg_lane = jnp.where(lo_half, lg_a, lg_b)
            q2 = q_ref[rs, cs].astype(F32)
            k2 = k_ref[rs, cs].astype(F32)
            zero = jnp.zeros_like(q2)
            qa = jnp.where(lo_half, q2, zero)
            qb = jnp.where(lo_half, zero, q2)
            k2b = k2.astype(BF16)
            scores = _nt_dot(jnp.concatenate([qa, qb], axis=0).astype(BF16), k2b)
            k_end = (k2 * jnp.exp((CHUNK - 1.0 - row) * lg_lane)).astype(BF16)
            q_start = jnp.exp((row + 1.0) * lg_lane)
            st_pair = st_ref[jp * LANE:(jp + 1) * LANE, :]
            st_b = st_pair.astype(BF16)
            new_rows = []
            for hh, (lg, qh) in enumerate(((lg_a, qa), (lg_b, qb))):
                h = 2 * jp + hh
                vh = v_ref[rs, h * RET_V_DIM:(h + 1) * RET_V_DIM]
                decay = jnp.where(rel >= 0, jnp.exp(jnp.maximum(rel, 0.0) * lg), 0.0)
                inner = _dot((scores[hh * CHUNK:(hh + 1) * CHUNK] * decay).astype(BF16), vh)
                cross = _dot((qh * q_start).astype(BF16), st_b)
                kv = _tn_dot(k_end, vh)
                half = slice(hh * 64, (hh + 1) * 64)
                new_rows.append(st_pair[half] * math.exp(CHUNK * lg) + kv[half])
                yh = inner + cross
                ms = jnp.mean(yh * yh, axis=-1, keepdims=True)
                hs = slice(h * RET_V_DIM, (h + 1) * RET_V_DIM)
                y_ref[rs, hs] = (yh * lax.rsqrt(ms + EPS) * gn_ref[:, hs]
                                 * jax.nn.silu(g_ref[rs, hs].astype(F32))).astype(BF16)
            st_ref[jp * LANE:(jp + 1) * LANE, :] = jnp.concatenate(new_rows, axis=0)


def _ret(qkr, proj, gain, batch, seq, ts):
    t = qkr.shape[0]
    ns = seq // ts
    half = RET_QK_WIDTH
    return pl.pallas_call(
        functools.partial(_ret_kernel, ts=ts),
        grid=(batch, ns),
        in_specs=[
            pl.BlockSpec((ts, half), lambda b, s: (b * ns + s, 2 * PB // half)),
            pl.BlockSpec((ts, half), lambda b, s: (b * ns + s, 2 * PB // half + 1)),
            pl.BlockSpec((ts, PB), lambda b, s: (b * ns + s, BLK_RV)),
            pl.BlockSpec((ts, PB), lambda b, s: (b * ns + s, BLK_RG)),
            pl.BlockSpec((1, RET_V_WIDTH), lambda b, s: (0, 0)),
        ],
        out_specs=pl.BlockSpec((ts, RET_V_WIDTH), lambda b, s: (b * ns + s, 0)),
        out_shape=_sds((t, RET_V_WIDTH), BF16),
        scratch_shapes=[pltpu.VMEM((RET_QK_WIDTH, RET_V_DIM), F32)],
        compiler_params=_cparams(("parallel", "arbitrary")),
        name="retention",
    )(qkr, qkr, proj, proj, gain)


def _outproj_kernel(ys_ref, yr_ref, x_ref, w_ref, g_ref, *rest, tm, dils):
    nb = len(dils)
    o_refs, l_refs = rest[0:nb], rest[nb:2 * nb]
    xo_ref, hn_ref, il_ref, ya_ref = rest[2 * nb:]
    for c in range(ATT_WIDTH // LANE):
        cs = slice(c * LANE, (c + 1) * LANE)
        outs, lses = [], []
        for k, dil in enumerate(dils):
            if dil == 1:
                outs.append(o_refs[k][0, :, cs].astype(F32))
                lses.append(l_refs[k][0, :, cs])
                continue
            for res in range(dil):
                rows = pl.ds(res, tm // dil, stride=dil)
                il_ref[2 * k, c, rows, :] = o_refs[k][res, :, cs].astype(F32)
                il_ref[2 * k + 1, c, rows, :] = l_refs[k][res, :, cs]
            outs.append(il_ref[2 * k, c])
            lses.append(il_ref[2 * k + 1, c])
        m = functools.reduce(jnp.maximum, lses)
        es = [jnp.exp(l - m) for l in lses]
        num = functools.reduce(lambda a, b: a + b, [e * o for e, o in zip(es, outs)])
        ya_ref[:, cs] = (num / functools.reduce(lambda a, b: a + b, es)).astype(BF16)
    acc = x_ref[...]
    acc = acc + _dot(ys_ref[...], w_ref[0:SSD_WIDTH, :])
    acc = acc + _dot(ya_ref[...], w_ref[SSD_WIDTH:SSD_WIDTH + ATT_WIDTH, :])
    acc = acc + _dot(yr_ref[...], w_ref[SSD_WIDTH + ATT_WIDTH:MIX_WIDTH, :])
    xo_ref[...] = acc
    ms = jnp.mean(acc * acc, axis=-1, keepdims=True)
    hn_ref[...] = ((acc * lax.rsqrt(ms + EPS)) * g_ref[...]).astype(BF16)


def _outproj(y_ssd, outs, lses, y_ret, x2d, w_out, gain, seq, tm):
    t = x2d.shape[0]
    nseq = seq // tm
    dils = tuple(o.shape[1] for o in outs)
    rows = lambda width: pl.BlockSpec((tm, width), lambda i: (i, 0))
    branch = lambda dil: pl.BlockSpec((None, dil, tm // dil, ATT_WIDTH), lambda i: (i // nseq, 0, i % nseq, 0))
    return pl.pallas_call(
        functools.partial(_outproj_kernel, tm=tm, dils=dils),
        grid=(t // tm,),
        in_specs=[rows(SSD_WIDTH), rows(RET_V_WIDTH), rows(D_MODEL),
                  pl.BlockSpec((MIX_WIDTH, D_MODEL), lambda i: (0, 0), pipeline_mode=pl.Buffered(1)),
                  pl.BlockSpec((1, D_MODEL), lambda i: (0, 0))] + [branch(d) for d in dils] * 2,
        out_specs=[rows(D_MODEL), rows(D_MODEL)],
        out_shape=[_sds((t, D_MODEL), F32), _sds((t, D_MODEL), BF16)],
        scratch_shapes=[pltpu.VMEM((2 * len(dils), ATT_WIDTH // LANE, tm, LANE), F32),
                        pltpu.VMEM((tm, ATT_WIDTH), BF16)],
        compiler_params=_cparams(("parallel",)),
        name="outproj",
    )(y_ssd, y_ret, x2d, w_out, gain, *outs, *lses)


def _ffn_kernel(hn_ref, x_ref, wg_ref, wu_ref, wd_ref, o_ref):
    @pl.when(pl.program_id(1) == 0)
    def _():
        o_ref[...] = x_ref[...]

    h = hn_ref[...]
    act = (jax.nn.silu(_dot(h, wg_ref[...])) * _dot(h, wu_ref[...])).astype(BF16)
    o_ref[...] += _dot(act, wd_ref[...])


def _ffn(hn, x2d, w_gate, w_up, w_down, tm, tf):
    t = x2d.shape[0]
    return pl.pallas_call(
        _ffn_kernel,
        grid=(t // tm, D_FF // tf),
        in_specs=[
            pl.BlockSpec((tm, D_MODEL), lambda i, j: (i, 0)),
            pl.BlockSpec((tm, D_MODEL), lambda i, j: (i, 0), pipeline_mode=pl.Buffered(1)),
            pl.BlockSpec((D_MODEL, tf), lambda i, j: (0, j)),
            pl.BlockSpec((D_MODEL, tf), lambda i, j: (0, j)),
            pl.BlockSpec((tf, D_MODEL), lambda i, j: (j, 0)),
        ],
        out_specs=pl.BlockSpec((tm, D_MODEL), lambda i, j: (i, 0)),
        out_shape=_sds((t, D_MODEL), F32),
        compiler_params=_cparams(("parallel", "arbitrary")),
        name="ffn",
    )(hn, x2d, w_gate, w_up, w_down)


def _tiles(seq):
    return dict(
        inproj=(min(512, seq), IN_MAIN // 2),
        prep=min(512, seq),
        scan=min(256, seq),
        att=512,
        outproj=min(512, seq),
        ffn=(min(1024, seq), 512),
    )


def _rope_tables(seq):
    pos = jnp.arange(seq, dtype=F32)
    inv = ROPE_THETA ** (-jnp.arange(0, HEAD_DIM, 2, dtype=F32) / HEAD_DIM)
    ang = pos[:, None] * inv[None, :]
    cos, sin = jnp.cos(ang), jnp.sin(ang)
    cosf = jnp.concatenate([cos, cos, cos, cos], axis=-1)
    sinf = jnp.concatenate([-sin, sin, -sin, sin], axis=-1)
    return cosf, sinf


def _head_sum_matrix():
    idx = np.arange(PB) // HEAD_DIM
    return jnp.asarray(idx[:, None] == idx[None, :], dtype=BF16)


def kernel(x, ln_mix, w_in, conv_w, conv_b, dt_bias, a_log, d_skip, ssd_norm, q_norm, k_norm, ret_norm, w_out,
           ln_ffn, w_gate, w_up, w_down):
    batch, seq, _ = x.shape
    depth = w_in.shape[0]
    t = batch * seq
    tiles = _tiles(seq)
    cosf, sinf = _rope_tables(seq)
    gmat = _head_sum_matrix()
    lane_pad = lambda v: jnp.pad(v.astype(F32), (0, LANE - v.shape[0]))[None, :]
    dils = tuple(dil for _, dil in DILATED_PAIRS)
    assert dils[0] == 1 and all(window // dil == ATT_BLOCK for window, dil in DILATED_PAIRS)

    x2d = x.reshape(t, D_MODEL)
    for i in range(depth):
        w_i = w_in[i]
        w_main = jnp.concatenate([w_i[:, :DT_COL], w_i[:, DT_COL + SSD_HEADS:]], axis=1).astype(BF16)
        w_dt = jnp.pad(w_i[:, DT_COL:DT_COL + SSD_HEADS], ((0, 0), (0, LANE - SSD_HEADS))).astype(BF16)
        proj, dt_raw = _inproj(x2d, ln_mix[i][None, :], w_main, w_dt, *tiles["inproj"])

        qkr, *qkv_dil = _prep(proj, cosf, sinf, jnp.tile(q_norm[i], ATT_HEADS)[None, :],
                              jnp.tile(k_norm[i], ATT_HEADS)[None, :], gmat, batch, seq, tiles["prep"], dils[1:])
        y_ssd = _ssd(proj, dt_raw, conv_w[i], conv_b[i][None, :], lane_pad(dt_bias[i]), lane_pad(a_log[i]),
                     jnp.repeat(d_skip[i], SSD_HEAD_DIM)[None, :], ssd_norm[i][None, :], batch, seq, tiles["scan"])
        branch_in = [(qkr.reshape(batch, 1, seq, QKR_WIDTH), proj.reshape(batch, 1, seq, IN_MAIN), BLK_AV)]
        branch_in += [(a, a, 2) for a in qkv_dil]
        outs, lses = zip(*[_att_branch(qk_arr, v_arr, v_blk, min(tiles["att"], qk_arr.shape[2]))
                           for qk_arr, v_arr, v_blk in branch_in])
        y_ret = _ret(qkr, proj, ret_norm[i][None, :], batch, seq, tiles["scan"])

        x2d, hn = _outproj(y_ssd, outs, lses, y_ret, x2d, w_out[i].astype(BF16), ln_ffn[i][None, :], seq,
                           tiles["outproj"])
        x2d = _ffn(hn, x2d, w_gate[i].astype(BF16), w_up[i].astype(BF16), w_down[i].astype(BF16), *tiles["ffn"])
    return x2d.reshape(batch, seq, D_MODEL)
```

```python
import functools
import math

import numpy as np
import jax
import jax.numpy as jnp
from jax import lax
from jax.experimental import pallas as pl
from jax.experimental.pallas import tpu as pltpu

F32 = jnp.float32
BF16 = jnp.bfloat16
NEG_INF = float("-inf")

D_MODEL = 2048
EPS = 1e-6
ROPE_THETA = 10000.0

SSD_HEADS = 16
SSD_HEAD_DIM = 64
SSD_WIDTH = SSD_HEADS * SSD_HEAD_DIM
SSD_GROUPS = 2
SSD_STATE = 128
SSD_CONV = 4
SSD_CONV_CH = SSD_WIDTH + 2 * SSD_GROUPS * SSD_STATE
CHUNK = 128

ATT_HEADS = 8
HEAD_DIM = 64
ATT_WIDTH = ATT_HEADS * HEAD_DIM
DILATED_PAIRS = ((128, 1), (512, 4), (2048, 16))
ATT_BLOCK = 128
STAT_LANES = 128 // ATT_HEADS

RET_HEADS = 4
RET_QK_WIDTH = RET_HEADS * 64
RET_V_DIM = 128
RET_V_WIDTH = RET_HEADS * RET_V_DIM

MIX_WIDTH = SSD_WIDTH + ATT_WIDTH + RET_V_WIDTH
D_FF = 5632
DT_COL = SSD_WIDTH + SSD_CONV_CH
IN_MAIN = DT_COL + 3 * ATT_WIDTH + 2 * RET_QK_WIDTH + 2 * RET_V_WIDTH

LANE = 128
SUBLANE = 8
PB = 512
BLK_BC = (2 * SSD_WIDTH) // PB
BLK_AQ, BLK_AK, BLK_AV = 5, 6, 7
BLK_RQK, BLK_RV, BLK_RG = 8, 9, 10
PROJ_BLKS = IN_MAIN // PB
QKR_WIDTH = 3 * PB
VMEM_LIMIT = 56 * 1024 * 1024


def _cparams(sem):
    return pltpu.CompilerParams(dimension_semantics=sem, vmem_limit_bytes=VMEM_LIMIT)


def _sds(shape, dtype):
    return jax.ShapeDtypeStruct(shape, dtype)


def _nt_dot(a, b):
    return lax.dot_general(a, b, (((1,), (1,)), ((), ())), preferred_element_type=F32)


def _tn_dot(a, b):
    return lax.dot_general(a, b, (((0,), (0,)), ((), ())), preferred_element_type=F32)


def _dot(a, b):
    return jnp.dot(a, b, preferred_element_type=F32)


def _split3_dot(lhs_bf16, x):
    hi = x.astype(BF16)
    r = x - hi.astype(F32)
    mid = r.astype(BF16)
    lo = (r - mid.astype(F32)).astype(BF16)
    return _dot(lhs_bf16, hi) + _dot(lhs_bf16, mid) + _dot(lhs_bf16, lo)


def _inproj_kernel(x_ref, g_ref, w_ref, wdt_ref, proj_ref, dt_ref, xn_ref):
    @pl.when(pl.program_id(1) == 0)
    def _():
        x = x_ref[...]
        ms = jnp.mean(x * x, axis=-1, keepdims=True)
        xn_ref[...] = ((x * lax.rsqrt(ms + EPS)) * g_ref[...]).astype(BF16)
        dt_ref[...] = _dot(xn_ref[...], wdt_ref[...])

    proj_ref[...] = _dot(xn_ref[...], w_ref[...]).astype(BF16)


def _inproj(x2d, gain, w_main, w_dt, tm, tn):
    t = x2d.shape[0]
    return pl.pallas_call(
        _inproj_kernel,
        grid=(t // tm, IN_MAIN // tn),
        in_specs=[
            pl.BlockSpec((tm, D_MODEL), lambda i, j: (i, 0)),
            pl.BlockSpec((1, D_MODEL), lambda i, j: (0, 0)),
            pl.BlockSpec((D_MODEL, tn), lambda i, j: (0, j)),
            pl.BlockSpec((D_MODEL, LANE), lambda i, j: (0, 0)),
        ],
        out_specs=[
            pl.BlockSpec((tm, tn), lambda i, j: (i, j)),
            pl.BlockSpec((tm, LANE), lambda i, j: (i, 0)),
        ],
        out_shape=[_sds((t, IN_MAIN), BF16), _sds((t, LANE), F32)],
        scratch_shapes=[pltpu.VMEM((tm, D_MODEL), BF16)],
        compiler_params=_cparams(("parallel", "arbitrary")),
        name="inproj",
    )(x2d, gain, w_main, w_dt)


def _prep_kernel(aq_ref, ak_ref, av_ref, rqk_ref, cos_ref, sin_ref, qg_ref, kg_ref, gm_ref,
                 tok_ref, *rest, tm, dils):
    nd = len(dils)
    dil_refs, slab_ref, stage_refs = rest[:nd], rest[nd], rest[nd + 1:]
    cosf = cos_ref[...]
    sinf = sin_ref[...]
    lane = lax.broadcasted_iota(jnp.int32, cosf.shape, 1)
    first_half = (lane & (HEAD_DIM // 2)) == 0

    def rope(xc):
        swapped = jnp.where(first_half, pltpu.roll(xc, LANE - HEAD_DIM // 2, axis=1),
                            pltpu.roll(xc, HEAD_DIM // 2, axis=1))
        return xc * cosf + swapped * sinf

    def head_rms(x_ref, gain_ref):
        x = x_ref[...].astype(F32)
        sq = x * x
        hi = sq.astype(BF16)
        lo = (sq - hi.astype(F32)).astype(BF16)
        ss = _dot(hi, gm_ref[...]) + _dot(lo, gm_ref[...])
        return (x * lax.rsqrt(ss * (1.0 / HEAD_DIM) + EPS)) * gain_ref[...]

    q = head_rms(aq_ref, qg_ref)
    k = head_rms(ak_ref, kg_ref)
    v = av_ref[...].astype(F32)
    r = rqk_ref[...].astype(F32)
    scale = HEAD_DIM ** -0.5
    n_lt = PB // LANE
    for c in range(n_lt):
        cs = slice(c * LANE, (c + 1) * LANE)
        qc = rope(q[:, cs]) * scale
        kc = rope(k[:, cs])
        rr = rope(r[:, cs])
        if c * LANE >= RET_QK_WIDTH:
            rr = rr * scale
        tok_ref[:, cs] = qc.astype(BF16)
        tok_ref[:, PB + c * LANE:PB + (c + 1) * LANE] = kc.astype(BF16)
        tok_ref[:, 2 * PB + c * LANE:2 * PB + (c + 1) * LANE] = rr.astype(BF16)
        for part, val in enumerate((qc, kc, v[:, cs])):
            slot = part * n_lt + c
            slab_ref[slot] = val
            col = slice(part * PB + c * LANE, part * PB + (c + 1) * LANE)
            parent = 1
            for lvl, (dil, d_ref) in enumerate(zip(dils, dil_refs)):
                ratio, rows = dil // parent, tm // dil
                for pres in range(parent):
                    for j in range(ratio):
                        take = pl.ds(j, rows, stride=ratio)
                        piece = slab_ref[slot, take, :] if lvl == 0 else stage_refs[lvl - 1][slot, pres, take, :]
                        res = pres + parent * j
                        d_ref[res, :, col] = piece.astype(BF16)
                        if lvl + 1 < nd:
                            stage_refs[lvl][slot, res] = piece
                parent = dil


def _prep(proj, cosf, sinf, qg, kg, gmat, batch, seq, tm, dils):
    t = proj.shape[0]
    nseq = seq // tm
    return pl.pallas_call(
        functools.partial(_prep_kernel, tm=tm, dils=dils),
        grid=(t // tm,),
        in_specs=[
            pl.BlockSpec((tm, PB), lambda i: (i, BLK_AQ)),
            pl.BlockSpec((tm, PB), lambda i: (i, BLK_AK)),
            pl.BlockSpec((tm, PB), lambda i: (i, BLK_AV)),
            pl.BlockSpec((tm, PB), lambda i: (i, BLK_RQK)),
            pl.BlockSpec((tm, LANE), lambda i: (i % nseq, 0)),
            pl.BlockSpec((tm, LANE), lambda i: (i % nseq, 0)),
            pl.BlockSpec((1, PB), lambda i: (0, 0)),
            pl.BlockSpec((1, PB), lambda i: (0, 0)),
            pl.BlockSpec((PB, PB), lambda i: (0, 0)),
        ],
        out_specs=[pl.BlockSpec((tm, QKR_WIDTH), lambda i: (i, 0))] + [
            pl.BlockSpec((None, dil, tm // dil, QKR_WIDTH), lambda i: (i // nseq, 0, i % nseq, 0)) for dil in dils],
        out_shape=[_sds((t, QKR_WIDTH), BF16)] + [_sds((batch, dil, seq // dil, QKR_WIDTH), BF16) for dil in dils],
        scratch_shapes=[pltpu.VMEM((3 * (PB // LANE), tm, LANE), F32)] + [
            pltpu.VMEM((3 * (PB // LANE), dil, tm // dil, LANE), F32) for dil in dils[:-1]],
        compiler_params=_cparams(("parallel",)),
        name="qkprep",
    )(proj, proj, proj, proj, cosf, sinf, qg, kg, gmat)


def _expand_heads(q, lo_half):
    rows = q.shape[0]
    tiles = []
    for jp in range(SSD_HEADS // 2):
        c0 = jnp.broadcast_to(q[:, 2 * jp:2 * jp + 1], (rows, LANE))
        c1 = jnp.broadcast_to(q[:, 2 * jp + 1:2 * jp + 2], (rows, LANE))
        tiles.append(jnp.where(lo_half, c0, c1))
    return jnp.concatenate(tiles, axis=1)


def _ssd_kernel(z_ref, xs_ref, bc_ref, dt_ref, cw_ref, cb_ref, dtb_ref, alog_ref, dsk_ref, gn_ref,
                y_ref, xpad_ref, st_ref, *, ts):
    @pl.when(pl.program_id(1) == 0)
    def _():
        xpad_ref[0:SUBLANE, :] = jnp.zeros((SUBLANE, SSD_CONV_CH), F32)
        st_ref[...] = jnp.zeros(st_ref.shape, F32)

    xpad_ref[SUBLANE:SUBLANE + ts, 0:SSD_WIDTH] = xs_ref[...].astype(F32)
    xpad_ref[SUBLANE:SUBLANE + ts, SSD_WIDTH:SSD_CONV_CH] = bc_ref[...].astype(F32)

    row = lax.broadcasted_iota(jnp.int32, (CHUNK, CHUNK), 0)
    col = lax.broadcasted_iota(jnp.int32, (CHUNK, CHUNK), 1)
    causal = row >= col
    ltri = jnp.where(causal, 1.0, 0.0).astype(BF16)
    lo_half = col < HEAD_DIM
    a_neg = -jnp.exp(alog_ref[...])
    gw = SSD_WIDTH // SSD_GROUPS

    for c in range(ts // CHUNK):
        r0 = c * CHUNK
        acc = cb_ref[...]
        for i in range(SSD_CONV):
            off = r0 + SUBLANE - (SSD_CONV - 1) + i
            acc = acc + cw_ref[i:i + 1, :] * xpad_ref[off:off + CHUNK, :]
        xc = jax.nn.silu(acc)
        xs = xc[:, 0:SSD_WIDTH]
        bmat = xc[:, SSD_WIDTH:SSD_WIDTH + SSD_GROUPS * SSD_STATE].astype(BF16)
        cmat = xc[:, SSD_WIDTH + SSD_GROUPS * SSD_STATE:SSD_CONV_CH].astype(BF16)

        dt = jax.nn.softplus(dt_ref[r0:r0 + CHUNK, :] + dtb_ref[...])
        acum = _split3_dot(ltri, dt * a_neg)
        acum_t = acum.T
        dt_t = dt.T
        a_last = acum[CHUNK - 1:CHUNK, :]
        w_end = _expand_heads(dt * jnp.exp(a_last - acum), lo_half)
        ea = _expand_heads(jnp.exp(acum), lo_half)
        xw = (xs * w_end).astype(BF16)
        xs_b = xs.astype(BF16)

        y_tiles = []
        for g in range(SSD_GROUPS):
            bg = bmat[:, g * SSD_STATE:(g + 1) * SSD_STATE]
            cg = cmat[:, g * SSD_STATE:(g + 1) * SSD_STATE]
            cbm = _nt_dot(cg, bg)
            st_prev = st_ref[:, g * gw:(g + 1) * gw]
            y_off = _dot(cg, st_prev.astype(BF16)) * ea[:, g * gw:(g + 1) * gw]
            chunk_state = _tn_dot(bg, xw[:, g * gw:(g + 1) * gw])
            st_ref[:, g * gw:(g + 1) * gw] = st_prev * ea[CHUNK - 1:CHUNK, g * gw:(g + 1) * gw] + chunk_state
            for hp in range(SSD_HEADS // SSD_GROUPS // 2):
                h0 = g * (SSD_HEADS // SSD_GROUPS) + 2 * hp
                mats = []
                for h in (h0, h0 + 1):
                    seg = jnp.broadcast_to(acum[:, h:h + 1], (CHUNK, CHUNK)) - acum_t[h:h + 1, :]
                    dec = jnp.exp(jnp.where(causal, seg, NEG_INF))
                    mats.append((cbm * dec * dt_t[h:h + 1, :]).astype(BF16))
                yd = _dot(jnp.concatenate(mats, axis=0), xs_b[:, h0 * HEAD_DIM:h0 * HEAD_DIM + LANE])
                lt = slice(hp * LANE, (hp + 1) * LANE)
                y_tiles.append(jnp.where(lo_half, yd[0:CHUNK], yd[CHUNK:2 * CHUNK]) + y_off[:, lt])
        y = jnp.concatenate(y_tiles, axis=1) + xs * dsk_ref[...]
        y = y * jax.nn.silu(z_ref[r0:r0 + CHUNK, :].astype(F32))
        for g in range(SSD_GROUPS):
            yg = y[:, g * gw:(g + 1) * gw]
            ms = jnp.mean(yg * yg, axis=-1, keepdims=True)
            y_ref[r0:r0 + CHUNK, g * gw:(g + 1) * gw] = (
                yg * lax.rsqrt(ms + EPS) * gn_ref[:, g * gw:(g + 1) * gw]).astype(BF16)

    xpad_ref[0:SUBLANE, :] = xpad_ref[ts:ts + SUBLANE, :]


def _ssd(proj, dt_raw, conv_w, conv_b, dt_bias, a_log, d_skip, gain, batch, seq, ts):
    t = proj.shape[0]
    ns = seq // ts
    full = lambda shape: pl.BlockSpec(shape, lambda b, s: (0, 0))
    return pl.pallas_call(
        functools.partial(_ssd_kernel, ts=ts),
        grid=(batch, ns),
        in_specs=[
            pl.BlockSpec((ts, SSD_WIDTH), lambda b, s: (b * ns + s, 0)),
            pl.BlockSpec((ts, SSD_WIDTH), lambda b, s: (b * ns + s, 1)),
            pl.BlockSpec((ts, PB), lambda b, s: (b * ns + s, BLK_BC)),
            pl.BlockSpec((ts, LANE), lambda b, s: (b * ns + s, 0)),
            full((SSD_CONV, SSD_CONV_CH)), full((1, SSD_CONV_CH)), full((1, LANE)), full((1, LANE)),
            full((1, SSD_WIDTH)), full((1, SSD_WIDTH)),
        ],
        out_specs=pl.BlockSpec((ts, SSD_WIDTH), lambda b, s: (b * ns + s, 0)),
        out_shape=_sds((t, SSD_WIDTH), BF16),
        scratch_shapes=[pltpu.VMEM((ts + SUBLANE, SSD_CONV_CH), F32), pltpu.VMEM((SSD_STATE, SSD_WIDTH), F32)],
        compiler_params=_cparams(("parallel", "arbitrary")),
        name="ssd",
    )(proj, proj, proj, dt_raw, conv_w, conv_b, dt_bias, a_log, d_skip, gain)


def _att_kernel(q_ref, kp_ref, kc_ref, vp_ref, vc_ref, o_ref, m_ref, d_ref, *, tq):
    n = pl.program_id(2)
    blk = ATT_BLOCK
    iq = lax.broadcasted_iota(jnp.int32, (2 * blk, 2 * blk), 0) & (blk - 1)
    ik = lax.broadcasted_iota(jnp.int32, (2 * blk, 2 * blk), 1)
    delta = ik - iq
    band = (delta >= 0) & (delta <= blk)
    bias = jnp.where(band, 0.0, NEG_INF)
    before_start = jnp.where(n > 0, 0.0, NEG_INF)
    bias_first = jnp.where(ik >= blk, bias, jnp.minimum(bias, before_start))
    lane = lax.broadcasted_iota(jnp.int32, (blk, LANE), 1)
    lo_half = lane < HEAD_DIM
    head_of_lane = lane // STAT_LANES

    for i in range(tq // blk):
        rs = slice(i * blk, (i + 1) * blk)
        m_c = jnp.zeros((blk, LANE), F32)
        d_c = jnp.zeros((blk, LANE), F32)
        for j in range(ATT_WIDTH // LANE):
            cs = slice(j * LANE, (j + 1) * LANE)
            q2 = q_ref[rs, cs]
            zero = jnp.zeros_like(q2)
            qq = jnp.concatenate([jnp.where(lo_half, q2, zero), jnp.where(lo_half, zero, q2)], axis=0)
            if i == 0:
                k2 = jnp.concatenate([kp_ref[:, cs], kc_ref[0:blk, cs]], axis=0)
                v2 = jnp.concatenate([vp_ref[:, cs], vc_ref[0:blk, cs]], axis=0)
            else:
                k2 = kc_ref[(i - 1) * blk:(i + 1) * blk, cs]
                v2 = vc_ref[(i - 1) * blk:(i + 1) * blk, cs]
            s = _nt_dot(qq, k2) + (bias_first if i == 0 else bias)
            m = jnp.max(s, axis=-1, keepdims=True)
            p = jnp.exp(s - m)
            den = jnp.sum(p, axis=-1, keepdims=True)
            pv = _dot(p.astype(BF16), v2)
            o_ref[rs, cs] = jnp.where(lo_half, pv[0:blk], pv[blk:2 * blk]).astype(BF16)
            for hh in range(2):
                sel = head_of_lane == 2 * j + hh
                m_c = jnp.where(sel, m[hh * blk:(hh + 1) * blk], m_c)
                d_c = jnp.where(sel, den[hh * blk:(hh + 1) * blk], d_c)
        m_ref[rs, :] = m_c
        d_ref[rs, :] = d_c


def _att_branch(qk_arr, v_arr, v_blk, tq):
    batch, dil, sub, _ = qk_arr.shape
    kb = tq // ATT_BLOCK
    tile = lambda blk: pl.BlockSpec((None, None, tq, PB), lambda b, r, n: (b, r, n, blk))
    stat = pl.BlockSpec((None, None, tq, LANE), lambda b, r, n: (b, r, n, 0))
    prev = lambda blk: pl.BlockSpec((None, None, ATT_BLOCK, PB),
                                    lambda b, r, n: (b, r, jnp.maximum(n * kb - 1, 0), blk))
    return pl.pallas_call(
        functools.partial(_att_kernel, tq=tq),
        grid=(batch, dil, sub // tq),
        in_specs=[tile(0), prev(1), tile(1), prev(v_blk), tile(v_blk)],
        out_specs=[tile(0), stat, stat],
        out_shape=[_sds((batch, dil, sub, ATT_WIDTH), BF16), _sds((batch, dil, sub, LANE), F32),
                   _sds((batch, dil, sub, LANE), F32)],
        compiler_params=_cparams(("parallel", "parallel", "arbitrary")),
        name=f"att_d{dil}",
    )(qk_arr, qk_arr, qk_arr, v_arr, v_arr)


def _ret_kernel(q_ref, k_ref, v_ref, g_ref, gn_ref, y_ref, st_ref, *, ts):
    @pl.when(pl.program_id(1) == 0)
    def _():
        st_ref[...] = jnp.zeros(st_ref.shape, F32)

    row_i = lax.broadcasted_iota(jnp.int32, (CHUNK, LANE), 0)
    col_i = lax.broadcasted_iota(jnp.int32, (CHUNK, LANE), 1)
    row = row_i.astype(F32)
    rel = (row_i - col_i).astype(F32)
    lo_half = col_i < 64
    log_gamma = [math.log1p(-(2.0 ** (-5.0 - h))) for h in range(RET_HEADS)]

    for c in range(ts // CHUNK):
        rs = slice(c * CHUNK, (c + 1) * CHUNK)
        for jp in range(RET_HEADS // 2):
            cs = slice(jp * LANE, (jp + 1) * LANE)
            lg_a, lg_b = log_gamma[2 * jp], log_gamma[2 * jp + 1]
            lg_lane = jnp.where(lo_half, lg_a, lg_b)
            q2 = q_ref[rs, cs].astype(F32)
            k2 = k_ref[rs, cs].astype(F32)
            zero = jnp.zeros_like(q2)
            qa = jnp.where(lo_half, q2, zero)
            qb = jnp.where(lo_half, zero, q2)
            k2b = k2.astype(BF16)
            scores = _nt_dot(jnp.concatenate([qa, qb], axis=0).astype(BF16), k2b)
            k_end = (k2 * jnp.exp((CHUNK - 1.0 - row) * lg_lane)).astype(BF16)
            q_start = jnp.exp((row + 1.0) * lg_lane)
            st_pair = st_ref[jp * LANE:(jp + 1) * LANE, :]
            st_b = st_pair.astype(BF16)
            new_rows = []
            for hh, (lg, qh) in enumerate(((lg_a, qa), (lg_b, qb))):
                h = 2 * jp + hh
                vh = v_ref[rs, h * RET_V_DIM:(h + 1) * RET_V_DIM]
                decay = jnp.where(rel >= 0, jnp.exp(jnp.maximum(rel, 0.0) * lg), 0.0)
                inner = _dot((scores[hh * CHUNK:(hh + 1) * CHUNK] * decay).astype(BF16), vh)
                cross = _dot((qh * q_start).astype(BF16), st_b)
                kv = _tn_dot(k_end, vh)
                half = slice(hh * 64, (hh + 1) * 64)
                new_rows.append(st_pair[half] * math.exp(CHUNK * lg) + kv[half])
                yh = inner + cross
                ms = jnp.mean(yh * yh, axis=-1, keepdims=True)
                hs = slice(h * RET_V_DIM, (h + 1) * RET_V_DIM)
                y_ref[rs, hs] = (yh * lax.rsqrt(ms + EPS) * gn_ref[:, hs]
                                 * jax.nn.silu(g_ref[rs, hs].astype(F32))).astype(BF16)
            st_ref[jp * LANE:(jp + 1) * LANE, :] = jnp.concatenate(new_rows, axis=0)


def _ret(qkr, proj, gain, batch, seq, ts):
    t = qkr.shape[0]
    ns = seq // ts
    half = RET_QK_WIDTH
    return pl.pallas_call(
        functools.partial(_ret_kernel, ts=ts),
        grid=(batch, ns),
        in_specs=[
            pl.BlockSpec((ts, half), lambda b, s: (b * ns + s, 2 * PB // half)),
            pl.BlockSpec((ts, half), lambda b, s: (b * ns + s, 2 * PB // half + 1)),
            pl.BlockSpec((ts, PB), lambda b, s: (b * ns + s, BLK_RV)),
            pl.BlockSpec((ts, PB), lambda b, s: (b * ns + s, BLK_RG)),
            pl.BlockSpec((1, RET_V_WIDTH), lambda b, s: (0, 0)),
        ],
        out_specs=pl.BlockSpec((ts, RET_V_WIDTH), lambda b, s: (b * ns + s, 0)),
        out_shape=_sds((t, RET_V_WIDTH), BF16),
        scratch_shapes=[pltpu.VMEM((RET_QK_WIDTH, RET_V_DIM), F32)],
        compiler_params=_cparams(("parallel", "arbitrary")),
        name="retention",
    )(qkr, qkr, proj, proj, gain)


def _outproj_kernel(ys_ref, yr_ref, x_ref, w_ref, g_ref, *rest, tm, dils):
    nb = len(dils)
    o_refs, m_refs, d_refs = rest[0:nb], rest[nb:2 * nb], rest[2 * nb:3 * nb]
    xo_ref, hn_ref, il_ref, st_ref, ya_ref = rest[3 * nb:]
    maxes, sums = [], []
    for k, dil in enumerate(dils):
        if dil == 1:
            maxes.append(m_refs[k][0])
            sums.append(d_refs[k][0])
            continue
        for res in range(dil):
            rows = pl.ds(res, tm // dil, stride=dil)
            st_ref[2 * k, rows, :] = m_refs[k][res]
            st_ref[2 * k + 1, rows, :] = d_refs[k][res]
        maxes.append(st_ref[2 * k])
        sums.append(st_ref[2 * k + 1])
    m_all = functools.reduce(jnp.maximum, maxes)
    es = [jnp.exp(m - m_all) for m in maxes]
    inv_z = 1.0 / functools.reduce(lambda a, b: a + b, [e * d for e, d in zip(es, sums)])
    wts = [e * inv_z for e in es]
    lo_half = lax.broadcasted_iota(jnp.int32, (tm, LANE), 1) < HEAD_DIM
    for c in range(ATT_WIDTH // LANE):
        cs = slice(c * LANE, (c + 1) * LANE)
        la, lb = 2 * c * STAT_LANES, (2 * c + 1) * STAT_LANES
        y_att = None
        for k, dil in enumerate(dils):
            if dil == 1:
                o = o_refs[k][0, :, cs].astype(F32)
            else:
                for res in range(dil):
                    il_ref[k, c, pl.ds(res, tm // dil, stride=dil), :] = o_refs[k][res, :, cs].astype(F32)
                o = il_ref[k, c]
            wk = jnp.where(lo_half, jnp.broadcast_to(wts[k][:, la:la + 1], (tm, LANE)),
                           jnp.broadcast_to(wts[k][:, lb:lb + 1], (tm, LANE)))
            y_att = wk * o if y_att is None else y_att + wk * o
        ya_ref[:, cs] = y_att.astype(BF16)
    acc = x_ref[...]
    acc = acc + _dot(ys_ref[...], w_ref[0:SSD_WIDTH, :])
    acc = acc + _dot(ya_ref[...], w_ref[SSD_WIDTH:SSD_WIDTH + ATT_WIDTH, :])
    acc = acc + _dot(yr_ref[...], w_ref[SSD_WIDTH + ATT_WIDTH:MIX_WIDTH, :])
    xo_ref[...] = acc
    ms = jnp.mean(acc * acc, axis=-1, keepdims=True)
    hn_ref[...] = ((acc * lax.rsqrt(ms + EPS)) * g_ref[...]).astype(BF16)


def _outproj(y_ssd, outs, maxes, sums, y_ret, x2d, w_out, gain, seq, tm):
    t = x2d.shape[0]
    nseq = seq // tm
    dils = tuple(o.shape[1] for o in outs)
    rows = lambda width: pl.BlockSpec((tm, width), lambda i: (i, 0))
    branch = lambda dil, width: pl.BlockSpec((None, dil, tm // dil, width), lambda i: (i // nseq, 0, i % nseq, 0))
    return pl.pallas_call(
        functools.partial(_outproj_kernel, tm=tm, dils=dils),
        grid=(t // tm,),
        in_specs=[rows(SSD_WIDTH), rows(RET_V_WIDTH), rows(D_MODEL),
                  pl.BlockSpec((MIX_WIDTH, D_MODEL), lambda i: (0, 0), pipeline_mode=pl.Buffered(1)),
                  pl.BlockSpec((1, D_MODEL), lambda i: (0, 0))]
                 + [branch(d, ATT_WIDTH) for d in dils] + [branch(d, LANE) for d in dils] * 2,
        out_specs=[rows(D_MODEL), rows(D_MODEL)],
        out_shape=[_sds((t, D_MODEL), F32), _sds((t, D_MODEL), BF16)],
        scratch_shapes=[pltpu.VMEM((len(dils), ATT_WIDTH // LANE, tm, LANE), F32),
                        pltpu.VMEM((2 * len(dils), tm, LANE), F32),
                        pltpu.VMEM((tm, ATT_WIDTH), BF16)],
        compiler_params=_cparams(("parallel",)),
        name="outproj",
    )(y_ssd, y_ret, x2d, w_out, gain, *outs, *maxes, *sums)


def _ffn_kernel(hn_ref, x_ref, wg_ref, wu_ref, wd_ref, o_ref):
    @pl.when(pl.program_id(1) == 0)
    def _():
        o_ref[...] = x_ref[...]

    h = hn_ref[...]
    act = (jax.nn.silu(_dot(h, wg_ref[...])) * _dot(h, wu_ref[...])).astype(BF16)
    o_ref[...] += _dot(act, wd_ref[...])


def _ffn(hn, x2d, w_gate, w_up, w_down, tm, tf):
    t = x2d.shape[0]
    return pl.pallas_call(
        _ffn_kernel,
        grid=(t // tm, D_FF // tf),
        in_specs=[
            pl.BlockSpec((tm, D_MODEL), lambda i, j: (i, 0)),
            pl.BlockSpec((tm, D_MODEL), lambda i, j: (i, 0), pipeline_mode=pl.Buffered(1)),
            pl.BlockSpec((D_MODEL, tf), lambda i, j: (0, j)),
            pl.BlockSpec((D_MODEL, tf), lambda i, j: (0, j)),
            pl.BlockSpec((tf, D_MODEL), lambda i, j: (j, 0)),
        ],
        out_specs=pl.BlockSpec((tm, D_MODEL), lambda i, j: (i, 0)),
        out_shape=_sds((t, D_MODEL), F32),
        compiler_params=_cparams(("parallel", "arbitrary")),
        name="ffn",
    )(hn, x2d, w_gate, w_up, w_down)


def _tiles(seq):
    return dict(
        inproj=(min(512, seq), IN_MAIN // 2),
        prep=min(512, seq),
        scan=min(256, seq),
        att=512,
        outproj=min(512, seq),
        ffn=(min(1024, seq), 512),
    )


def _rope_tables(seq):
    pos = jnp.arange(seq, dtype=F32)
    inv = ROPE_THETA ** (-jnp.arange(0, HEAD_DIM, 2, dtype=F32) / HEAD_DIM)
    ang = pos[:, None] * inv[None, :]
    cos, sin = jnp.cos(ang), jnp.sin(ang)
    cosf = jnp.concatenate([cos, cos, cos, cos], axis=-1)
    sinf = jnp.concatenate([-sin, sin, -sin, sin], axis=-1)
    return cosf, sinf


def _head_sum_matrix():
    idx = np.arange(PB) // HEAD_DIM
    return jnp.asarray(idx[:, None] == idx[None, :], dtype=BF16)


def kernel(x, ln_mix, w_in, conv_w, conv_b, dt_bias, a_log, d_skip, ssd_norm, q_norm, k_norm, ret_norm, w_out,
           ln_ffn, w_gate, w_up, w_down):
    batch, seq, _ = x.shape
    depth = w_in.shape[0]
    t = batch * seq
    tiles = _tiles(seq)
    cosf, sinf = _rope_tables(seq)
    gmat = _head_sum_matrix()
    lane_pad = lambda v: jnp.pad(v.astype(F32), (0, LANE - v.shape[0]))[None, :]
    dils = tuple(dil for _, dil in DILATED_PAIRS)
    assert dils[0] == 1 and all(window // dil == ATT_BLOCK for window, dil in DILATED_PAIRS)

    x2d = x.reshape(t, D_MODEL)
    for i in range(depth):
        w_i = w_in[i]
        w_main = jnp.concatenate([w_i[:, :DT_COL], w_i[:, DT_COL + SSD_HEADS:]], axis=1).astype(BF16)
        w_dt = jnp.pad(w_i[:, DT_COL:DT_COL + SSD_HEADS], ((0, 0), (0, LANE - SSD_HEADS))).astype(BF16)
        proj, dt_raw = _inproj(x2d, ln_mix[i][None, :], w_main, w_dt, *tiles["inproj"])

        qkr, *qkv_dil = _prep(proj, cosf, sinf, jnp.tile(q_norm[i], ATT_HEADS)[None, :],
                              jnp.tile(k_norm[i], ATT_HEADS)[None, :], gmat, batch, seq, tiles["prep"], dils[1:])
        y_ssd = _ssd(proj, dt_raw, conv_w[i], conv_b[i][None, :], lane_pad(dt_bias[i]), lane_pad(a_log[i]),
                     jnp.repeat(d_skip[i], SSD_HEAD_DIM)[None, :], ssd_norm[i][None, :], batch, seq, tiles["scan"])
        branch_in = [(qkr.reshape(batch, 1, seq, QKR_WIDTH), proj.reshape(batch, 1, seq, IN_MAIN), BLK_AV)]
        branch_in += [(a, a, 2) for a in qkv_dil]
        outs, maxes, sums = zip(*[_att_branch(qk_arr, v_arr, v_blk, min(tiles["att"], qk_arr.shape[2]))
                                  for qk_arr, v_arr, v_blk in branch_in])
        y_ret = _ret(qkr, proj, ret_norm[i][None, :], batch, seq, tiles["scan"])

        x2d, hn = _outproj(y_ssd, outs, maxes, sums, y_ret, x2d, w_out[i].astype(BF16), ln_ffn[i][None, :], seq,
                           tiles["outproj"])
        x2d = _ffn(hn, x2d, w_gate[i].astype(BF16), w_up[i].astype(BF16), w_down[i].astype(BF16), *tiles["ffn"])
    return x2d.reshape(batch, seq, D_MODEL)
```

```python
import functools
import math

import numpy as np
import jax
import jax.numpy as jnp
from jax import lax
from jax.experimental import pallas as pl
from jax.experimental.pallas import tpu as pltpu

F32 = jnp.float32
BF16 = jnp.bfloat16
NEG_INF = float("-inf")

D_MODEL = 2048
EPS = 1e-6
ROPE_THETA = 10000.0

SSD_HEADS = 16
SSD_HEAD_DIM = 64
SSD_WIDTH = SSD_HEADS * SSD_HEAD_DIM
SSD_GROUPS = 2
SSD_STATE = 128
SSD_CONV = 4
SSD_CONV_CH = SSD_WIDTH + 2 * SSD_GROUPS * SSD_STATE
CHUNK = 128

ATT_HEADS = 8
HEAD_DIM = 64
ATT_WIDTH = ATT_HEADS * HEAD_DIM
DILATED_PAIRS = ((128, 1), (512, 4), (2048, 16))
ATT_BLOCK = 128
STAT_LANES = 128 // ATT_HEADS

RET_HEADS = 4
RET_QK_WIDTH = RET_HEADS * 64
RET_V_DIM = 128
RET_V_WIDTH = RET_HEADS * RET_V_DIM

MIX_WIDTH = SSD_WIDTH + ATT_WIDTH + RET_V_WIDTH
D_FF = 5632
DT_COL = SSD_WIDTH + SSD_CONV_CH
IN_MAIN = DT_COL + 3 * ATT_WIDTH + 2 * RET_QK_WIDTH + 2 * RET_V_WIDTH

LANE = 128
SUBLANE = 8
MXU_COLS = 256
SSD_SLOTS_PER_CHUNK = 3
PB = 512
BLK_AQ, BLK_AK, BLK_AV = 5, 6, 7
BLK_RQK, BLK_RV, BLK_RG = 8, 9, 10
QKR_WIDTH = 3 * PB
VMEM_LIMIT = 56 * 1024 * 1024


def _cparams(sem):
    return pltpu.CompilerParams(dimension_semantics=sem, vmem_limit_bytes=VMEM_LIMIT)


def _sds(shape, dtype):
    return jax.ShapeDtypeStruct(shape, dtype)


def _nt_dot(a, b):
    return lax.dot_general(a, b, (((1,), (1,)), ((), ())), preferred_element_type=F32)


def _tn_dot(a, b):
    return lax.dot_general(a, b, (((0,), (0,)), ((), ())), preferred_element_type=F32)


def _dot(a, b):
    return jnp.dot(a, b, preferred_element_type=F32)


def _split3_dot(lhs_bf16, x):
    hi = x.astype(BF16)
    r = x - hi.astype(F32)
    mid = r.astype(BF16)
    lo = (r - mid.astype(F32)).astype(BF16)
    return _dot(lhs_bf16, hi) + _dot(lhs_bf16, mid) + _dot(lhs_bf16, lo)


def _inproj_ssd_kernel(x_ref, g_ref, w_ref, wdt_ref, cw_ref, cb_ref, dtb_ref, alog_ref, dsk_ref, gn_ref,
                       proj_ref, y_ref, xn_ref, src_ref, dt_ref, xpad_ref, st_ref, *, tm, tiles_per_seq):
    @pl.when(pl.program_id(1) == 0)
    def _():
        @pl.when(pl.program_id(0) % tiles_per_seq == 0)
        def _():
            xpad_ref[0:SUBLANE, :] = jnp.zeros((SUBLANE, SSD_CONV_CH), F32)
            st_ref[...] = jnp.zeros(st_ref.shape, F32)

        x = x_ref[...]
        ms = jnp.mean(x * x, axis=-1, keepdims=True)
        xn_ref[...] = ((x * lax.rsqrt(ms + EPS)) * g_ref[...]).astype(BF16)
        dt_ref[...] = _dot(xn_ref[...], wdt_ref[...])
        tile = _dot(xn_ref[...], w_ref[...]).astype(BF16)
        proj_ref[...] = tile
        src_ref[...] = tile[:, 0:DT_COL]

    @pl.when(pl.program_id(1) == 1)
    def _():
        n_pieces = w_ref.shape[1] // MXU_COLS
        n_slots = SSD_SLOTS_PER_CHUNK * (tm // CHUNK)
        slot = iter(range(n_slots))

        def matmul_pieces():
            k = next(slot)
            for p in range((n_pieces * k) // n_slots, (n_pieces * (k + 1)) // n_slots):
                ps = slice(p * MXU_COLS, (p + 1) * MXU_COLS)
                proj_ref[:, ps] = _dot(xn_ref[...], w_ref[:, ps]).astype(BF16)

        _ssd_tile(src_ref, dt_ref, cw_ref, cb_ref, dtb_ref, alog_ref, dsk_ref, gn_ref, y_ref, xpad_ref, st_ref,
                  ts=tm, between=matmul_pieces)


def _inproj_ssd(x2d, gain, w_main, w_dt, conv_w, conv_b, dt_bias, a_log, d_skip, ssd_gain, seq, tm):
    t = x2d.shape[0]
    tn = IN_MAIN // 2
    assert tn >= DT_COL and tn % MXU_COLS == 0
    full = lambda shape: pl.BlockSpec(shape, lambda i, j: (0, 0))
    return pl.pallas_call(
        functools.partial(_inproj_ssd_kernel, tm=tm, tiles_per_seq=seq // tm),
        grid=(t // tm, 2),
        in_specs=[
            pl.BlockSpec((tm, D_MODEL), lambda i, j: (i, 0)),
            full((1, D_MODEL)),
            pl.BlockSpec((D_MODEL, tn), lambda i, j: (0, j)),
            full((D_MODEL, LANE)),
            full((SSD_CONV, SSD_CONV_CH)), full((1, SSD_CONV_CH)), full((1, LANE)), full((1, LANE)),
            full((1, SSD_WIDTH)), full((1, SSD_WIDTH)),
        ],
        out_specs=[
            pl.BlockSpec((tm, tn), lambda i, j: (i, j)),
            pl.BlockSpec((tm, SSD_WIDTH), lambda i, j: (i, 0)),
        ],
        out_shape=[_sds((t, IN_MAIN), BF16), _sds((t, SSD_WIDTH), BF16)],
        scratch_shapes=[pltpu.VMEM((tm, D_MODEL), BF16), pltpu.VMEM((tm, DT_COL), BF16), pltpu.VMEM((tm, LANE), F32),
                        pltpu.VMEM((tm + SUBLANE, SSD_CONV_CH), F32), pltpu.VMEM((SSD_STATE, SSD_WIDTH), F32)],
        compiler_params=_cparams(("arbitrary", "arbitrary")),
        name="inproj_ssd",
    )(x2d, gain, w_main, w_dt, conv_w, conv_b, dt_bias, a_log, d_skip, ssd_gain)


def _prep_kernel(aq_ref, ak_ref, av_ref, rqk_ref, cos_ref, sin_ref, qg_ref, kg_ref, gm_ref,
                 tok_ref, *rest, tm, dils):
    nd = len(dils)
    dil_refs, slab_ref, stage_refs = rest[:nd], rest[nd], rest[nd + 1:]
    cosf = cos_ref[...]
    sinf = sin_ref[...]
    lane = lax.broadcasted_iota(jnp.int32, cosf.shape, 1)
    first_half = (lane & (HEAD_DIM // 2)) == 0

    def rope(xc):
        swapped = jnp.where(first_half, pltpu.roll(xc, LANE - HEAD_DIM // 2, axis=1),
                            pltpu.roll(xc, HEAD_DIM // 2, axis=1))
        return xc * cosf + swapped * sinf

    def head_rms(x_ref, gain_ref):
        x = x_ref[...].astype(F32)
        sq = x * x
        hi = sq.astype(BF16)
        lo = (sq - hi.astype(F32)).astype(BF16)
        ss = _dot(hi, gm_ref[...]) + _dot(lo, gm_ref[...])
        return (x * lax.rsqrt(ss * (1.0 / HEAD_DIM) + EPS)) * gain_ref[...]

    q = head_rms(aq_ref, qg_ref)
    k = head_rms(ak_ref, kg_ref)
    v = av_ref[...].astype(F32)
    r = rqk_ref[...].astype(F32)
    scale = HEAD_DIM ** -0.5
    n_lt = PB // LANE
    for c in range(n_lt):
        cs = slice(c * LANE, (c + 1) * LANE)
        qc = rope(q[:, cs]) * scale
        kc = rope(k[:, cs])
        rr = rope(r[:, cs])
        if c * LANE >= RET_QK_WIDTH:
            rr = rr * scale
        tok_ref[:, cs] = qc.astype(BF16)
        tok_ref[:, PB + c * LANE:PB + (c + 1) * LANE] = kc.astype(BF16)
        tok_ref[:, 2 * PB + c * LANE:2 * PB + (c + 1) * LANE] = rr.astype(BF16)
        for part, val in enumerate((qc, kc, v[:, cs])):
            slot = part * n_lt + c
            slab_ref[slot] = val
            col = slice(part * PB + c * LANE, part * PB + (c + 1) * LANE)
            parent = 1
            for lvl, (dil, d_ref) in enumerate(zip(dils, dil_refs)):
                ratio, rows = dil // parent, tm // dil
                for pres in range(parent):
                    for j in range(ratio):
                        take = pl.ds(j, rows, stride=ratio)
                        piece = slab_ref[slot, take, :] if lvl == 0 else stage_refs[lvl - 1][slot, pres, take, :]
                        res = pres + parent * j
                        d_ref[res, :, col] = piece.astype(BF16)
                        if lvl + 1 < nd:
                            stage_refs[lvl][slot, res] = piece
                parent = dil


def _prep(proj, cosf, sinf, qg, kg, gmat, batch, seq, tm, dils):
    t = proj.shape[0]
    nseq = seq // tm
    return pl.pallas_call(
        functools.partial(_prep_kernel, tm=tm, dils=dils),
        grid=(t // tm,),
        in_specs=[
            pl.BlockSpec((tm, PB), lambda i: (i, BLK_AQ)),
            pl.BlockSpec((tm, PB), lambda i: (i, BLK_AK)),
            pl.BlockSpec((tm, PB), lambda i: (i, BLK_AV)),
            pl.BlockSpec((tm, PB), lambda i: (i, BLK_RQK)),
            pl.BlockSpec((tm, LANE), lambda i: (i % nseq, 0)),
            pl.BlockSpec((tm, LANE), lambda i: (i % nseq, 0)),
            pl.BlockSpec((1, PB), lambda i: (0, 0)),
            pl.BlockSpec((1, PB), lambda i: (0, 0)),
            pl.BlockSpec((PB, PB), lambda i: (0, 0)),
        ],
        out_specs=[pl.BlockSpec((tm, QKR_WIDTH), lambda i: (i, 0))] + [
            pl.BlockSpec((None, dil, tm // dil, QKR_WIDTH), lambda i: (i // nseq, 0, i % nseq, 0)) for dil in dils],
        out_shape=[_sds((t, QKR_WIDTH), BF16)] + [_sds((batch, dil, seq // dil, QKR_WIDTH), BF16) for dil in dils],
        scratch_shapes=[pltpu.VMEM((3 * (PB // LANE), tm, LANE), F32)] + [
            pltpu.VMEM((3 * (PB // LANE), dil, tm // dil, LANE), F32) for dil in dils[:-1]],
        compiler_params=_cparams(("parallel",)),
        name="qkprep",
    )(proj, proj, proj, proj, cosf, sinf, qg, kg, gmat)


def _expand_heads(q, lo_half):
    rows = q.shape[0]
    tiles = []
    for jp in range(SSD_HEADS // 2):
        c0 = jnp.broadcast_to(q[:, 2 * jp:2 * jp + 1], (rows, LANE))
        c1 = jnp.broadcast_to(q[:, 2 * jp + 1:2 * jp + 2], (rows, LANE))
        tiles.append(jnp.where(lo_half, c0, c1))
    return jnp.concatenate(tiles, axis=1)


def _ssd_tile(src_ref, dt_ref, cw_ref, cb_ref, dtb_ref, alog_ref, dsk_ref, gn_ref, y_ref, xpad_ref, st_ref,
              *, ts, between):
    xpad_ref[SUBLANE:SUBLANE + ts, :] = src_ref[:, SSD_WIDTH:SSD_WIDTH + SSD_CONV_CH].astype(F32)

    row = lax.broadcasted_iota(jnp.int32, (CHUNK, CHUNK), 0)
    col = lax.broadcasted_iota(jnp.int32, (CHUNK, CHUNK), 1)
    causal = row >= col
    ltri = jnp.where(causal, 1.0, 0.0).astype(BF16)
    lo_half = col < HEAD_DIM
    a_neg = -jnp.exp(alog_ref[...])
    gw = SSD_WIDTH // SSD_GROUPS

    for c in range(ts // CHUNK):
        r0 = c * CHUNK
        acc = cb_ref[...]
        for i in range(SSD_CONV):
            off = r0 + SUBLANE - (SSD_CONV - 1) + i
            acc = acc + cw_ref[i:i + 1, :] * xpad_ref[off:off + CHUNK, :]
        xc = jax.nn.silu(acc)
        xs = xc[:, 0:SSD_WIDTH]
        bmat = xc[:, SSD_WIDTH:SSD_WIDTH + SSD_GROUPS * SSD_STATE].astype(BF16)
        cmat = xc[:, SSD_WIDTH + SSD_GROUPS * SSD_STATE:SSD_CONV_CH].astype(BF16)

        dt = jax.nn.softplus(dt_ref[r0:r0 + CHUNK, :] + dtb_ref[...])
        acum = _split3_dot(ltri, dt * a_neg)
        acum_t = acum.T
        dt_t = dt.T
        a_last = acum[CHUNK - 1:CHUNK, :]
        w_end = _expand_heads(dt * jnp.exp(a_last - acum), lo_half)
        ea = _expand_heads(jnp.exp(acum), lo_half)
        xw = (xs * w_end).astype(BF16)
        xs_b = xs.astype(BF16)

        y_tiles = []
        between()
        for g in range(SSD_GROUPS):
            bg = bmat[:, g * SSD_STATE:(g + 1) * SSD_STATE]
            cg = cmat[:, g * SSD_STATE:(g + 1) * SSD_STATE]
            cbm = _nt_dot(cg, bg)
            st_prev = st_ref[:, g * gw:(g + 1) * gw]
            y_off = _dot(cg, st_prev.astype(BF16)) * ea[:, g * gw:(g + 1) * gw]
            chunk_state = _tn_dot(bg, xw[:, g * gw:(g + 1) * gw])
            st_ref[:, g * gw:(g + 1) * gw] = st_prev * ea[CHUNK - 1:CHUNK, g * gw:(g + 1) * gw] + chunk_state
            for hp in range(SSD_HEADS // SSD_GROUPS // 2):
                h0 = g * (SSD_HEADS // SSD_GROUPS) + 2 * hp
                mats = []
                for h in (h0, h0 + 1):
                    seg = jnp.broadcast_to(acum[:, h:h + 1], (CHUNK, CHUNK)) - acum_t[h:h + 1, :]
                    dec = jnp.exp(jnp.where(causal, seg, NEG_INF))
                    mats.append((cbm * dec * dt_t[h:h + 1, :]).astype(BF16))
                yd = _dot(jnp.concatenate(mats, axis=0), xs_b[:, h0 * HEAD_DIM:h0 * HEAD_DIM + LANE])
                lt = slice(hp * LANE, (hp + 1) * LANE)
                y_tiles.append(jnp.where(lo_half, yd[0:CHUNK], yd[CHUNK:2 * CHUNK]) + y_off[:, lt])
        between()
        y = jnp.concatenate(y_tiles, axis=1) + xs * dsk_ref[...]
        y = y * jax.nn.silu(src_ref[r0:r0 + CHUNK, 0:SSD_WIDTH].astype(F32))
        for g in range(SSD_GROUPS):
            yg = y[:, g * gw:(g + 1) * gw]
            ms = jnp.mean(yg * yg, axis=-1, keepdims=True)
            y_ref[r0:r0 + CHUNK, g * gw:(g + 1) * gw] = (
                yg * lax.rsqrt(ms + EPS) * gn_ref[:, g * gw:(g + 1) * gw]).astype(BF16)
        between()

    xpad_ref[0:SUBLANE, :] = xpad_ref[ts:ts + SUBLANE, :]


def _att_kernel(q_ref, kp_ref, kc_ref, vp_ref, vc_ref, o_ref, m_ref, d_ref, *, tq):
    n = pl.program_id(2)
    blk = ATT_BLOCK
    iq = lax.broadcasted_iota(jnp.int32, (2 * blk, 2 * blk), 0) & (blk - 1)
    ik = lax.broadcasted_iota(jnp.int32, (2 * blk, 2 * blk), 1)
    delta = ik - iq
    band = (delta >= 0) & (delta <= blk)
    bias = jnp.where(band, 0.0, NEG_INF)
    before_start = jnp.where(n > 0, 0.0, NEG_INF)
    bias_first = jnp.where(ik >= blk, bias, jnp.minimum(bias, before_start))
    lane = lax.broadcasted_iota(jnp.int32, (blk, LANE), 1)
    lo_half = lane < HEAD_DIM
    head_of_lane = lane // STAT_LANES

    for i in range(tq // blk):
        rs = slice(i * blk, (i + 1) * blk)
        m_c = jnp.zeros((blk, LANE), F32)
        d_c = jnp.zeros((blk, LANE), F32)
        for j in range(ATT_WIDTH // LANE):
            cs = slice(j * LANE, (j + 1) * LANE)
            q2 = q_ref[rs, cs]
            zero = jnp.zeros_like(q2)
            qq = jnp.concatenate([jnp.where(lo_half, q2, zero), jnp.where(lo_half, zero, q2)], axis=0)
            if i == 0:
                k2 = jnp.concatenate([kp_ref[:, cs], kc_ref[0:blk, cs]], axis=0)
                v2 = jnp.concatenate([vp_ref[:, cs], vc_ref[0:blk, cs]], axis=0)
            else:
                k2 = kc_ref[(i - 1) * blk:(i + 1) * blk, cs]
                v2 = vc_ref[(i - 1) * blk:(i + 1) * blk, cs]
            s = _nt_dot(qq, k2) + (bias_first if i == 0 else bias)
            m = jnp.max(s, axis=-1, keepdims=True)
            p = jnp.exp(s - m)
            den = jnp.sum(p, axis=-1, keepdims=True)
            pv = _dot(p.astype(BF16), v2)
            o_ref[rs, cs] = jnp.where(lo_half, pv[0:blk], pv[blk:2 * blk]).astype(BF16)
            for hh in range(2):
                sel = head_of_lane == 2 * j + hh
                m_c = jnp.where(sel, m[hh * blk:(hh + 1) * blk], m_c)
                d_c = jnp.where(sel, den[hh * blk:(hh + 1) * blk], d_c)
        m_ref[rs, :] = m_c
        d_ref[rs, :] = d_c


def _att_branch(qk_arr, v_arr, v_blk, tq):
    batch, dil, sub, _ = qk_arr.shape
    kb = tq // ATT_BLOCK
    tile = lambda blk: pl.BlockSpec((None, None, tq, PB), lambda b, r, n: (b, r, n, blk))
    stat = pl.BlockSpec((None, None, tq, LANE), lambda b, r, n: (b, r, n, 0))
    prev = lambda blk: pl.BlockSpec((None, None, ATT_BLOCK, PB),
                                    lambda b, r, n: (b, r, jnp.maximum(n * kb - 1, 0), blk))
    return pl.pallas_call(
        functools.partial(_att_kernel, tq=tq),
        grid=(batch, dil, sub // tq),
        in_specs=[tile(0), prev(1), tile(1), prev(v_blk), tile(v_blk)],
        out_specs=[tile(0), stat, stat],
        out_shape=[_sds((batch, dil, sub, ATT_WIDTH), BF16), _sds((batch, dil, sub, LANE), F32),
                   _sds((batch, dil, sub, LANE), F32)],
        compiler_params=_cparams(("parallel", "parallel", "arbitrary")),
        name=f"att_d{dil}",
    )(qk_arr, qk_arr, qk_arr, v_arr, v_arr)


def _ret_kernel(q_ref, k_ref, v_ref, g_ref, gn_ref, y_ref, st_ref, *, ts):
    @pl.when(pl.program_id(1) == 0)
    def _():
        st_ref[...] = jnp.zeros(st_ref.shape, F32)

    row_i = lax.broadcasted_iota(jnp.int32, (CHUNK, LANE), 0)
    col_i = lax.broadcasted_iota(jnp.int32, (CHUNK, LANE), 1)
    row = row_i.astype(F32)
    rel = (row_i - col_i).astype(F32)
    lo_half = col_i < 64
    log_gamma = [math.log1p(-(2.0 ** (-5.0 - h))) for h in range(RET_HEADS)]

    for c in range(ts // CHUNK):
        rs = slice(c * CHUNK, (c + 1) * CHUNK)
        for jp in range(RET_HEADS // 2):
            cs = slice(jp * LANE, (jp + 1) * LANE)
            lg_a, lg_b = log_gamma[2 * jp], log_gamma[2 * jp + 1]
            lg_lane = jnp.where(lo_half, lg_a, lg_b)
            q2 = q_ref[rs, cs].astype(F32)
            k2 = k_ref[rs, cs].astype(F32)
            zero = jnp.zeros_like(q2)
            qa = jnp.where(lo_half, q2, zero)
            qb = jnp.where(lo_half, zero, q2)
            k2b = k2.astype(BF16)
            scores = _nt_dot(jnp.concatenate([qa, qb], axis=0).astype(BF16), k2b)
            k_end = (k2 * jnp.exp((CHUNK - 1.0 - row) * lg_lane)).astype(BF16)
            q_start = jnp.exp((row + 1.0) * lg_lane)
            st_pair = st_ref[jp * LANE:(jp + 1) * LANE, :]
            st_b = st_pair.astype(BF16)
            new_rows = []
            for hh, (lg, qh) in enumerate(((lg_a, qa), (lg_b, qb))):
                h = 2 * jp + hh
                vh = v_ref[rs, h * RET_V_DIM:(h + 1) * RET_V_DIM]
                decay = jnp.where(rel >= 0, jnp.exp(jnp.maximum(rel, 0.0) * lg), 0.0)
                inner = _dot((scores[hh * CHUNK:(hh + 1) * CHUNK] * decay).astype(BF16), vh)
                cross = _dot((qh * q_start).astype(BF16), st_b)
                kv = _tn_dot(k_end, vh)
                half = slice(hh * 64, (hh + 1) * 64)
                new_rows.append(st_pair[half] * math.exp(CHUNK * lg) + kv[half])
                yh = inner + cross
                ms = jnp.mean(yh * yh, axis=-1, keepdims=True)
                hs = slice(h * RET_V_DIM, (h + 1) * RET_V_DIM)
                y_ref[rs, hs] = (yh * lax.rsqrt(ms + EPS) * gn_ref[:, hs]
                                 * jax.nn.silu(g_ref[rs, hs].astype(F32))).astype(BF16)
            st_ref[jp * LANE:(jp + 1) * LANE, :] = jnp.concatenate(new_rows, axis=0)


def _ret(qkr, proj, gain, batch, seq, ts):
    t = qkr.shape[0]
    ns = seq // ts
    half = RET_QK_WIDTH
    return pl.pallas_call(
        functools.partial(_ret_kernel, ts=ts),
        grid=(batch, ns),
        in_specs=[
            pl.BlockSpec((ts, half), lambda b, s: (b * ns + s, 2 * PB // half)),
            pl.BlockSpec((ts, half), lambda b, s: (b * ns + s, 2 * PB // half + 1)),
            pl.BlockSpec((ts, PB), lambda b, s: (b * ns + s, BLK_RV)),
            pl.BlockSpec((ts, PB), lambda b, s: (b * ns + s, BLK_RG)),
            pl.BlockSpec((1, RET_V_WIDTH), lambda b, s: (0, 0)),
        ],
        out_specs=pl.BlockSpec((ts, RET_V_WIDTH), lambda b, s: (b * ns + s, 0)),
        out_shape=_sds((t, RET_V_WIDTH), BF16),
        scratch_shapes=[pltpu.VMEM((RET_QK_WIDTH, RET_V_DIM), F32)],
        compiler_params=_cparams(("parallel", "arbitrary")),
        name="retention",
    )(qkr, qkr, proj, proj, gain)


def _outproj_kernel(ys_ref, yr_ref, x_ref, w_ref, g_ref, *rest, tm, dils):
    nb = len(dils)
    o_refs, m_refs, d_refs = rest[0:nb], rest[nb:2 * nb], rest[2 * nb:3 * nb]
    xo_ref, hn_ref, il_ref, st_ref, ya_ref = rest[3 * nb:]
    maxes, sums = [], []
    for k, dil in enumerate(dils):
        if dil == 1:
            maxes.append(m_refs[k][0])
            sums.append(d_refs[k][0])
            continue
        for res in range(dil):
            rows = pl.ds(res, tm // dil, stride=dil)
            st_ref[2 * k, rows, :] = m_refs[k][res]
            st_ref[2 * k + 1, rows, :] = d_refs[k][res]
        maxes.append(st_ref[2 * k])
        sums.append(st_ref[2 * k + 1])
    m_all = functools.reduce(jnp.maximum, maxes)
    es = [jnp.exp(m - m_all) for m in maxes]
    inv_z = 1.0 / functools.reduce(lambda a, b: a + b, [e * d for e, d in zip(es, sums)])
    wts = [e * inv_z for e in es]
    lo_half = lax.broadcasted_iota(jnp.int32, (tm, LANE), 1) < HEAD_DIM
    for c in range(ATT_WIDTH // LANE):
        cs = slice(c * LANE, (c + 1) * LANE)
        la, lb = 2 * c * STAT_LANES, (2 * c + 1) * STAT_LANES
        y_att = None
        for k, dil in enumerate(dils):
            if dil == 1:
                o = o_refs[k][0, :, cs].astype(F32)
            else:
                for res in range(dil):
                    il_ref[k, c, pl.ds(res, tm // dil, stride=dil), :] = o_refs[k][res, :, cs].astype(F32)
                o = il_ref[k, c]
            wk = jnp.where(lo_half, jnp.broadcast_to(wts[k][:, la:la + 1], (tm, LANE)),
                           jnp.broadcast_to(wts[k][:, lb:lb + 1], (tm, LANE)))
            y_att = wk * o if y_att is None else y_att + wk * o
        ya_ref[:, cs] = y_att.astype(BF16)
    acc = x_ref[...]
    acc = acc + _dot(ys_ref[...], w_ref[0:SSD_WIDTH, :])
    acc = acc + _dot(ya_ref[...], w_ref[SSD_WIDTH:SSD_WIDTH + ATT_WIDTH, :])
    acc = acc + _dot(yr_ref[...], w_ref[SSD_WIDTH + ATT_WIDTH:MIX_WIDTH, :])
    xo_ref[...] = acc
    ms = jnp.mean(acc * acc, axis=-1, keepdims=True)
    hn_ref[...] = ((acc * lax.rsqrt(ms + EPS)) * g_ref[...]).astype(BF16)


def _outproj(y_ssd, outs, maxes, sums, y_ret, x2d, w_out, gain, seq, tm):
    t = x2d.shape[0]
    nseq = seq // tm
    dils = tuple(o.shape[1] for o in outs)
    rows = lambda width: pl.BlockSpec((tm, width), lambda i: (i, 0))
    branch = lambda dil, width: pl.BlockSpec((None, dil, tm // dil, width), lambda i: (i // nseq, 0, i % nseq, 0))
    return pl.pallas_call(
        functools.partial(_outproj_kernel, tm=tm, dils=dils),
        grid=(t // tm,),
        in_specs=[rows(SSD_WIDTH), rows(RET_V_WIDTH), rows(D_MODEL),
                  pl.BlockSpec((MIX_WIDTH, D_MODEL), lambda i: (0, 0), pipeline_mode=pl.Buffered(1)),
                  pl.BlockSpec((1, D_MODEL), lambda i: (0, 0))]
                 + [branch(d, ATT_WIDTH) for d in dils] + [branch(d, LANE) for d in dils] * 2,
        out_specs=[rows(D_MODEL), rows(D_MODEL)],
        out_shape=[_sds((t, D_MODEL), F32), _sds((t, D_MODEL), BF16)],
        scratch_shapes=[pltpu.VMEM((len(dils), ATT_WIDTH // LANE, tm, LANE), F32),
                        pltpu.VMEM((2 * len(dils), tm, LANE), F32),
                        pltpu.VMEM((tm, ATT_WIDTH), BF16)],
        compiler_params=_cparams(("parallel",)),
        name="outproj",
    )(y_ssd, y_ret, x2d, w_out, gain, *outs, *maxes, *sums)


def _ffn_kernel(hn_ref, x_ref, wg_ref, wu_ref, wd_ref, o_ref):
    @pl.when(pl.program_id(1) == 0)
    def _():
        o_ref[...] = x_ref[...]

    h = hn_ref[...]
    act = (jax.nn.silu(_dot(h, wg_ref[...])) * _dot(h, wu_ref[...])).astype(BF16)
    o_ref[...] += _dot(act, wd_ref[...])


def _ffn(hn, x2d, w_gate, w_up, w_down, tm, tf):
    t = x2d.shape[0]
    return pl.pallas_call(
        _ffn_kernel,
        grid=(t // tm, D_FF // tf),
        in_specs=[
            pl.BlockSpec((tm, D_MODEL), lambda i, j: (i, 0)),
            pl.BlockSpec((tm, D_MODEL), lambda i, j: (i, 0), pipeline_mode=pl.Buffered(1)),
            pl.BlockSpec((D_MODEL, tf), lambda i, j: (0, j)),
            pl.BlockSpec((D_MODEL, tf), lambda i, j: (0, j)),
            pl.BlockSpec((tf, D_MODEL), lambda i, j: (j, 0)),
        ],
        out_specs=pl.BlockSpec((tm, D_MODEL), lambda i, j: (i, 0)),
        out_shape=_sds((t, D_MODEL), F32),
        compiler_params=_cparams(("parallel", "arbitrary")),
        name="ffn",
    )(hn, x2d, w_gate, w_up, w_down)


def _tiles(seq):
    return dict(
        inproj=min(512, seq),
        prep=min(512, seq),
        scan=min(256, seq),
        att=512,
        outproj=min(512, seq),
        ffn=(min(1024, seq), 512),
    )


def _rope_tables(seq):
    pos = jnp.arange(seq, dtype=F32)
    inv = ROPE_THETA ** (-jnp.arange(0, HEAD_DIM, 2, dtype=F32) / HEAD_DIM)
    ang = pos[:, None] * inv[None, :]
    cos, sin = jnp.cos(ang), jnp.sin(ang)
    cosf = jnp.concatenate([cos, cos, cos, cos], axis=-1)
    sinf = jnp.concatenate([-sin, sin, -sin, sin], axis=-1)
    return cosf, sinf


def _head_sum_matrix():
    idx = np.arange(PB) // HEAD_DIM
    return jnp.asarray(idx[:, None] == idx[None, :], dtype=BF16)


def kernel(x, ln_mix, w_in, conv_w, conv_b, dt_bias, a_log, d_skip, ssd_norm, q_norm, k_norm, ret_norm, w_out,
           ln_ffn, w_gate, w_up, w_down):
    batch, seq, _ = x.shape
    depth = w_in.shape[0]
    t = batch * seq
    tiles = _tiles(seq)
    cosf, sinf = _rope_tables(seq)
    gmat = _head_sum_matrix()
    lane_pad = lambda v: jnp.pad(v.astype(F32), (0, LANE - v.shape[0]))[None, :]
    dils = tuple(dil for _, dil in DILATED_PAIRS)
    assert dils[0] == 1 and all(window // dil == ATT_BLOCK for window, dil in DILATED_PAIRS)

    x2d = x.reshape(t, D_MODEL)
    for i in range(depth):
        w_i = w_in[i]
        w_main = jnp.concatenate([w_i[:, :DT_COL], w_i[:, DT_COL + SSD_HEADS:]], axis=1).astype(BF16)
        w_dt = jnp.pad(w_i[:, DT_COL:DT_COL + SSD_HEADS], ((0, 0), (0, LANE - SSD_HEADS))).astype(BF16)
        proj, y_ssd = _inproj_ssd(x2d, ln_mix[i][None, :], w_main, w_dt, conv_w[i], conv_b[i][None, :],
                                  lane_pad(dt_bias[i]), lane_pad(a_log[i]), jnp.repeat(d_skip[i], SSD_HEAD_DIM)[None, :],
                                  ssd_norm[i][None, :], seq, tiles["inproj"])

        qkr, *qkv_dil = _prep(proj, cosf, sinf, jnp.tile(q_norm[i], ATT_HEADS)[None, :],
                              jnp.tile(k_norm[i], ATT_HEADS)[None, :], gmat, batch, seq, tiles["prep"], dils[1:])
        branch_in = [(qkr.reshape(batch, 1, seq, QKR_WIDTH), proj.reshape(batch, 1, seq, IN_MAIN), BLK_AV)]
        branch_in += [(a, a, 2) for a in qkv_dil]
        outs, maxes, sums = zip(*[_att_branch(qk_arr, v_arr, v_blk, min(tiles["att"], qk_arr.shape[2]))
                                  for qk_arr, v_arr, v_blk in branch_in])
        y_ret = _ret(qkr, proj, ret_norm[i][None, :], batch, seq, tiles["scan"])

        x2d, hn = _outproj(y_ssd, outs, maxes, sums, y_ret, x2d, w_out[i].astype(BF16), ln_ffn[i][None, :], seq,
                           tiles["outproj"])
        x2d = _ffn(hn, x2d, w_gate[i].astype(BF16), w_up[i].astype(BF16), w_down[i].astype(BF16), *tiles["ffn"])
    return x2d.reshape(batch, seq, D_MODEL)
```

```python
import functools
import math

import numpy as np
import jax
import jax.numpy as jnp
from jax import lax
from jax.experimental import pallas as pl
from jax.experimental.pallas import tpu as pltpu

F32 = jnp.float32
BF16 = jnp.bfloat16
NEG_INF = float("-inf")
LOG2E = math.log2(math.e)

D_MODEL = 2048
EPS = 1e-6
ROPE_THETA = 10000.0

SSD_HEADS = 16
SSD_HEAD_DIM = 64
SSD_WIDTH = SSD_HEADS * SSD_HEAD_DIM
SSD_GROUPS = 2
SSD_STATE = 128
SSD_CONV = 4
SSD_CONV_CH = SSD_WIDTH + 2 * SSD_GROUPS * SSD_STATE
CHUNK = 128

ATT_HEADS = 8
HEAD_DIM = 64
ATT_WIDTH = ATT_HEADS * HEAD_DIM
DILATED_PAIRS = ((128, 1), (512, 4), (2048, 16))
ATT_BLOCK = 128
STAT_LANES = 128 // ATT_HEADS

RET_HEADS = 4
RET_QK_DIM = 64
RET_QK_WIDTH = RET_HEADS * RET_QK_DIM
RET_V_DIM = 128
RET_V_WIDTH = RET_HEADS * RET_V_DIM

MIX_WIDTH = SSD_WIDTH + ATT_WIDTH + RET_V_WIDTH
D_FF = 5632
DT_COL = SSD_WIDTH + SSD_CONV_CH
IN_MAIN = DT_COL + 3 * ATT_WIDTH + 2 * RET_QK_WIDTH + 2 * RET_V_WIDTH

LANE = 128
SUBLANE = 8
MXU_COLS = 256
SSD_SLOTS_PER_CHUNK = 3
PB = 512
BLK_AQ, BLK_AK, BLK_AV = 5, 6, 7
BLK_RQK, BLK_RV, BLK_RG = 8, 9, 10
QKR_WIDTH = 3 * PB
VMEM_LIMIT = 56 * 1024 * 1024


def _cparams(sem):
    return pltpu.CompilerParams(dimension_semantics=sem, vmem_limit_bytes=VMEM_LIMIT)


def _sds(shape, dtype):
    return jax.ShapeDtypeStruct(shape, dtype)


def _nt_dot(a, b):
    return lax.dot_general(a, b, (((1,), (1,)), ((), ())), preferred_element_type=F32)


def _tn_dot(a, b):
    return lax.dot_general(a, b, (((0,), (0,)), ((), ())), preferred_element_type=F32)


def _dot(a, b):
    return jnp.dot(a, b, preferred_element_type=F32)


def _split3_dot(lhs_bf16, x):
    hi = x.astype(BF16)
    r = x - hi.astype(F32)
    mid = r.astype(BF16)
    lo = (r - mid.astype(F32)).astype(BF16)
    return _dot(lhs_bf16, hi) + _dot(lhs_bf16, mid) + _dot(lhs_bf16, lo)


def _inproj_ssd_kernel(x_ref, g_ref, w_ref, wdt_ref, cw_ref, cb_ref, dtb_ref, alog_ref, dsk_ref, gn_ref,
                       proj_ref, y_ref, xn_ref, src_ref, dt_ref, xpad_ref, st_ref, *, tm, tiles_per_seq):
    @pl.when(pl.program_id(1) == 0)
    def _():
        @pl.when(pl.program_id(0) % tiles_per_seq == 0)
        def _():
            xpad_ref[0:SUBLANE, :] = jnp.zeros((SUBLANE, SSD_CONV_CH), F32)
            st_ref[...] = jnp.zeros(st_ref.shape, F32)

        x = x_ref[...]
        ms = jnp.mean(x * x, axis=-1, keepdims=True)
        xn_ref[...] = ((x * lax.rsqrt(ms + EPS)) * g_ref[...]).astype(BF16)
        dt_ref[...] = _dot(xn_ref[...], wdt_ref[...])
        tile = _dot(xn_ref[...], w_ref[...]).astype(BF16)
        proj_ref[...] = tile
        src_ref[...] = tile[:, 0:DT_COL]

    @pl.when(pl.program_id(1) == 1)
    def _():
        n_pieces = w_ref.shape[1] // MXU_COLS
        n_slots = SSD_SLOTS_PER_CHUNK * (tm // CHUNK)
        slot = iter(range(n_slots))

        def matmul_pieces():
            k = next(slot)
            for p in range((n_pieces * k) // n_slots, (n_pieces * (k + 1)) // n_slots):
                ps = slice(p * MXU_COLS, (p + 1) * MXU_COLS)
                proj_ref[:, ps] = _dot(xn_ref[...], w_ref[:, ps]).astype(BF16)

        _ssd_tile(src_ref, dt_ref, cw_ref, cb_ref, dtb_ref, alog_ref, dsk_ref, gn_ref, y_ref, xpad_ref, st_ref,
                  ts=tm, between=matmul_pieces)


def _inproj_ssd(x2d, gain, w_main, w_dt, conv_w, conv_b, dt_bias, a_log, d_skip, ssd_gain, seq, tm):
    t = x2d.shape[0]
    tn = IN_MAIN // 2
    assert tn >= DT_COL and tn % MXU_COLS == 0
    full = lambda shape: pl.BlockSpec(shape, lambda i, j: (0, 0))
    return pl.pallas_call(
        functools.partial(_inproj_ssd_kernel, tm=tm, tiles_per_seq=seq // tm),
        grid=(t // tm, 2),
        in_specs=[
            pl.BlockSpec((tm, D_MODEL), lambda i, j: (i, 0)),
            full((1, D_MODEL)),
            pl.BlockSpec((D_MODEL, tn), lambda i, j: (0, j)),
            full((D_MODEL, LANE)),
            full((SSD_CONV, SSD_CONV_CH)), full((1, SSD_CONV_CH)), full((1, LANE)), full((1, LANE)),
            full((1, SSD_WIDTH)), full((1, SSD_WIDTH)),
        ],
        out_specs=[
            pl.BlockSpec((tm, tn), lambda i, j: (i, j)),
            pl.BlockSpec((tm, SSD_WIDTH), lambda i, j: (i, 0)),
        ],
        out_shape=[_sds((t, IN_MAIN), BF16), _sds((t, SSD_WIDTH), BF16)],
        scratch_shapes=[pltpu.VMEM((tm, D_MODEL), BF16), pltpu.VMEM((tm, DT_COL), BF16), pltpu.VMEM((tm, LANE), F32),
                        pltpu.VMEM((tm + SUBLANE, SSD_CONV_CH), F32), pltpu.VMEM((SSD_STATE, SSD_WIDTH), F32)],
        compiler_params=_cparams(("arbitrary", "arbitrary")),
        name="inproj_ssd",
    )(x2d, gain, w_main, w_dt, conv_w, conv_b, dt_bias, a_log, d_skip, ssd_gain)


def _prep_kernel(aq_ref, ak_ref, av_ref, rqk_ref, cos_ref, sin_ref, qg_ref, kg_ref, gm_ref,
                 tok_ref, *rest, tm, dils):
    nd = len(dils)
    dil_refs, slab_ref, stage_refs = rest[:nd], rest[nd], rest[nd + 1:]
    cosf = cos_ref[...]
    sinf = sin_ref[...]
    lane = lax.broadcasted_iota(jnp.int32, cosf.shape, 1)
    first_half = (lane & (HEAD_DIM // 2)) == 0

    def rope(xc):
        swapped = jnp.where(first_half, pltpu.roll(xc, LANE - HEAD_DIM // 2, axis=1),
                            pltpu.roll(xc, HEAD_DIM // 2, axis=1))
        return xc * cosf + swapped * sinf

    def head_rms(x_ref, gain_ref):
        x = x_ref[...].astype(F32)
        sq = x * x
        hi = sq.astype(BF16)
        lo = (sq - hi.astype(F32)).astype(BF16)
        ss = _dot(hi, gm_ref[...]) + _dot(lo, gm_ref[...])
        return (x * lax.rsqrt(ss * (1.0 / HEAD_DIM) + EPS)) * gain_ref[...]

    q = head_rms(aq_ref, qg_ref)
    k = head_rms(ak_ref, kg_ref)
    v = av_ref[...].astype(F32)
    r = rqk_ref[...].astype(F32)
    scale = HEAD_DIM ** -0.5
    n_lt = PB // LANE
    for c in range(n_lt):
        cs = slice(c * LANE, (c + 1) * LANE)
        qc = rope(q[:, cs]) * (scale * LOG2E)
        kc = rope(k[:, cs])
        rr = rope(r[:, cs])
        if c * LANE >= RET_QK_WIDTH:
            rr = rr * scale
        tok_ref[:, cs] = qc.astype(BF16)
        tok_ref[:, PB + c * LANE:PB + (c + 1) * LANE] = kc.astype(BF16)
        tok_ref[:, 2 * PB + c * LANE:2 * PB + (c + 1) * LANE] = rr.astype(BF16)
        for part, val in enumerate((qc, kc, v[:, cs])):
            slot = part * n_lt + c
            slab_ref[slot] = val
            col = slice(part * PB + c * LANE, part * PB + (c + 1) * LANE)
            parent = 1
            for lvl, (dil, d_ref) in enumerate(zip(dils, dil_refs)):
                ratio, rows = dil // parent, tm // dil
                for pres in range(parent):
                    for j in range(ratio):
                        take = pl.ds(j, rows, stride=ratio)
                        piece = slab_ref[slot, take, :] if lvl == 0 else stage_refs[lvl - 1][slot, pres, take, :]
                        res = pres + parent * j
                        d_ref[res, :, col] = piece.astype(BF16)
                        if lvl + 1 < nd:
                            stage_refs[lvl][slot, res] = piece
                parent = dil


def _prep(proj, cosf, sinf, qg, kg, gmat, batch, seq, tm, dils):
    t = proj.shape[0]
    nseq = seq // tm
    return pl.pallas_call(
        functools.partial(_prep_kernel, tm=tm, dils=dils),
        grid=(t // tm,),
        in_specs=[
            pl.BlockSpec((tm, PB), lambda i: (i, BLK_AQ)),
            pl.BlockSpec((tm, PB), lambda i: (i, BLK_AK)),
            pl.BlockSpec((tm, PB), lambda i: (i, BLK_AV)),
            pl.BlockSpec((tm, PB), lambda i: (i, BLK_RQK)),
            pl.BlockSpec((tm, LANE), lambda i: (i % nseq, 0)),
            pl.BlockSpec((tm, LANE), lambda i: (i % nseq, 0)),
            pl.BlockSpec((1, PB), lambda i: (0, 0)),
            pl.BlockSpec((1, PB), lambda i: (0, 0)),
            pl.BlockSpec((PB, PB), lambda i: (0, 0)),
        ],
        out_specs=[pl.BlockSpec((tm, QKR_WIDTH), lambda i: (i, 0))] + [
            pl.BlockSpec((None, dil, tm // dil, QKR_WIDTH), lambda i: (i // nseq, 0, i % nseq, 0)) for dil in dils],
        out_shape=[_sds((t, QKR_WIDTH), BF16)] + [_sds((batch, dil, seq // dil, QKR_WIDTH), BF16) for dil in dils],
        scratch_shapes=[pltpu.VMEM((3 * (PB // LANE), tm, LANE), F32)] + [
            pltpu.VMEM((3 * (PB // LANE), dil, tm // dil, LANE), F32) for dil in dils[:-1]],
        compiler_params=_cparams(("parallel",)),
        name="qkprep",
    )(proj, proj, proj, proj, cosf, sinf, qg, kg, gmat)


def _expand_heads(q, lo_half):
    rows = q.shape[0]
    tiles = []
    for jp in range(SSD_HEADS // 2):
        c0 = jnp.broadcast_to(q[:, 2 * jp:2 * jp + 1], (rows, LANE))
        c1 = jnp.broadcast_to(q[:, 2 * jp + 1:2 * jp + 2], (rows, LANE))
        tiles.append(jnp.where(lo_half, c0, c1))
    return jnp.concatenate(tiles, axis=1)


def _ssd_tile(src_ref, dt_ref, cw_ref, cb_ref, dtb_ref, alog_ref, dsk_ref, gn_ref, y_ref, xpad_ref, st_ref,
              *, ts, between):
    xpad_ref[SUBLANE:SUBLANE + ts, :] = src_ref[:, SSD_WIDTH:SSD_WIDTH + SSD_CONV_CH].astype(F32)

    row = lax.broadcasted_iota(jnp.int32, (CHUNK, CHUNK), 0)
    col = lax.broadcasted_iota(jnp.int32, (CHUNK, CHUNK), 1)
    causal = row >= col
    ltri = jnp.where(causal, 1.0, 0.0).astype(BF16)
    lo_half = col < HEAD_DIM
    a_neg = -jnp.exp(alog_ref[...])
    gw = SSD_WIDTH // SSD_GROUPS

    for c in range(ts // CHUNK):
        r0 = c * CHUNK
        acc = cb_ref[...]
        for i in range(SSD_CONV):
            off = r0 + SUBLANE - (SSD_CONV - 1) + i
            acc = acc + cw_ref[i:i + 1, :] * xpad_ref[off:off + CHUNK, :]
        xc = jax.nn.silu(acc)
        xs = xc[:, 0:SSD_WIDTH]
        bmat = xc[:, SSD_WIDTH:SSD_WIDTH + SSD_GROUPS * SSD_STATE].astype(BF16)
        cmat = xc[:, SSD_WIDTH + SSD_GROUPS * SSD_STATE:SSD_CONV_CH].astype(BF16)

        dt = jax.nn.softplus(dt_ref[r0:r0 + CHUNK, :] + dtb_ref[...])
        acum = _split3_dot(ltri, dt * a_neg)
        acum_t = acum.T
        dt_t = dt.T
        a_last = acum[CHUNK - 1:CHUNK, :]
        w_end = _expand_heads(dt * jnp.exp(a_last - acum), lo_half)
        ea = _expand_heads(jnp.exp(acum), lo_half)
        xw = (xs * w_end).astype(BF16)
        xs_b = xs.astype(BF16)

        y_tiles = []
        between()
        for g in range(SSD_GROUPS):
            bg = bmat[:, g * SSD_STATE:(g + 1) * SSD_STATE]
            cg = cmat[:, g * SSD_STATE:(g + 1) * SSD_STATE]
            cbm = _nt_dot(cg, bg)
            st_prev = st_ref[:, g * gw:(g + 1) * gw]
            y_off = _dot(cg, st_prev.astype(BF16)) * ea[:, g * gw:(g + 1) * gw]
            chunk_state = _tn_dot(bg, xw[:, g * gw:(g + 1) * gw])
            st_ref[:, g * gw:(g + 1) * gw] = st_prev * ea[CHUNK - 1:CHUNK, g * gw:(g + 1) * gw] + chunk_state
            for hp in range(SSD_HEADS // SSD_GROUPS // 2):
                h0 = g * (SSD_HEADS // SSD_GROUPS) + 2 * hp
                mats = []
                for h in (h0, h0 + 1):
                    seg = jnp.broadcast_to(acum[:, h:h + 1], (CHUNK, CHUNK)) - acum_t[h:h + 1, :]
                    dec = jnp.exp(jnp.where(causal, seg, NEG_INF))
                    mats.append((cbm * dec * dt_t[h:h + 1, :]).astype(BF16))
                yd = _dot(jnp.concatenate(mats, axis=0), xs_b[:, h0 * HEAD_DIM:h0 * HEAD_DIM + LANE])
                lt = slice(hp * LANE, (hp + 1) * LANE)
                y_tiles.append(jnp.where(lo_half, yd[0:CHUNK], yd[CHUNK:2 * CHUNK]) + y_off[:, lt])
        between()
        y = jnp.concatenate(y_tiles, axis=1) + xs * dsk_ref[...]
        y = y * jax.nn.silu(src_ref[r0:r0 + CHUNK, 0:SSD_WIDTH].astype(F32))
        for g in range(SSD_GROUPS):
            yg = y[:, g * gw:(g + 1) * gw]
            ms = jnp.mean(yg * yg, axis=-1, keepdims=True)
            y_ref[r0:r0 + CHUNK, g * gw:(g + 1) * gw] = (
                yg * lax.rsqrt(ms + EPS) * gn_ref[:, g * gw:(g + 1) * gw]).astype(BF16)
        between()

    xpad_ref[0:SUBLANE, :] = xpad_ref[ts:ts + SUBLANE, :]


def _att_kernel(q_ref, kp_ref, kc_ref, vp_ref, vc_ref, o_ref, m_ref, d_ref, *, tq):
    n = pl.program_id(2)
    blk = ATT_BLOCK
    iq = lax.broadcasted_iota(jnp.int32, (2 * blk, 2 * blk), 0) & (blk - 1)
    ik = lax.broadcasted_iota(jnp.int32, (2 * blk, 2 * blk), 1)
    delta = ik - iq
    band = (delta >= 0) & (delta <= blk)
    bias = jnp.where(band, 0.0, NEG_INF)
    before_start = jnp.where(n > 0, 0.0, NEG_INF)
    bias_first = jnp.where(ik >= blk, bias, jnp.minimum(bias, before_start))
    lane = lax.broadcasted_iota(jnp.int32, (blk, LANE), 1)
    lo_half = lane < HEAD_DIM
    head_of_lane = lane // STAT_LANES

    for i in range(tq // blk):
        rs = slice(i * blk, (i + 1) * blk)
        m_c = jnp.zeros((blk, LANE), F32)
        d_c = jnp.zeros((blk, LANE), F32)
        for j in range(ATT_WIDTH // LANE):
            cs = slice(j * LANE, (j + 1) * LANE)
            q2 = q_ref[rs, cs]
            zero = jnp.zeros_like(q2)
            qq = jnp.concatenate([jnp.where(lo_half, q2, zero), jnp.where(lo_half, zero, q2)], axis=0)
            if i == 0:
                k2 = jnp.concatenate([kp_ref[:, cs], kc_ref[0:blk, cs]], axis=0)
                v2 = jnp.concatenate([vp_ref[:, cs], vc_ref[0:blk, cs]], axis=0)
            else:
                k2 = kc_ref[(i - 1) * blk:(i + 1) * blk, cs]
                v2 = vc_ref[(i - 1) * blk:(i + 1) * blk, cs]
            s = _nt_dot(qq, k2) + (bias_first if i == 0 else bias)
            m = jnp.max(s, axis=-1, keepdims=True)
            p = jnp.exp2(s - m)
            den = jnp.sum(p, axis=-1, keepdims=True)
            pv = _dot(p.astype(BF16), v2)
            o_ref[rs, cs] = jnp.where(lo_half, pv[0:blk], pv[blk:2 * blk]).astype(BF16)
            for hh in range(2):
                sel = head_of_lane == 2 * j + hh
                m_c = jnp.where(sel, m[hh * blk:(hh + 1) * blk], m_c)
                d_c = jnp.where(sel, den[hh * blk:(hh + 1) * blk], d_c)
        m_ref[rs, :] = m_c
        d_ref[rs, :] = d_c


def _att_branch(qk_arr, v_arr, v_blk, tq):
    batch, dil, sub, _ = qk_arr.shape
    kb = tq // ATT_BLOCK
    tile = lambda blk: pl.BlockSpec((None, None, tq, PB), lambda b, r, n: (b, r, n, blk))
    stat = pl.BlockSpec((None, None, tq, LANE), lambda b, r, n: (b, r, n, 0))
    prev = lambda blk: pl.BlockSpec((None, None, ATT_BLOCK, PB),
                                    lambda b, r, n: (b, r, jnp.maximum(n * kb - 1, 0), blk))
    return pl.pallas_call(
        functools.partial(_att_kernel, tq=tq),
        grid=(batch, dil, sub // tq),
        in_specs=[tile(0), prev(1), tile(1), prev(v_blk), tile(v_blk)],
        out_specs=[tile(0), stat, stat],
        out_shape=[_sds((batch, dil, sub, ATT_WIDTH), BF16), _sds((batch, dil, sub, LANE), F32),
                   _sds((batch, dil, sub, LANE), F32)],
        compiler_params=_cparams(("parallel", "parallel", "arbitrary")),
        name=f"att_d{dil}",
    )(qk_arr, qk_arr, qk_arr, v_arr, v_arr)


def _ret_tile(q_ref, k_ref, v_ref, g_ref, gn_ref, y_ref, st_ref, *, ts):
    row_i = lax.broadcasted_iota(jnp.int32, (CHUNK, LANE), 0)
    col_i = lax.broadcasted_iota(jnp.int32, (CHUNK, LANE), 1)
    row = row_i.astype(F32)
    rel = (row_i - col_i).astype(F32)
    lo_half = col_i < RET_QK_DIM
    log_gamma = [math.log1p(-(2.0 ** (-5.0 - h))) for h in range(RET_HEADS)]
    decay = [jnp.where(rel >= 0, jnp.exp(jnp.maximum(rel, 0.0) * lg), 0.0) for lg in log_gamma]
    lg_lane = [jnp.where(lo_half, log_gamma[2 * jp], log_gamma[2 * jp + 1]) for jp in range(RET_HEADS // 2)]
    to_end = [jnp.exp((CHUNK - 1.0 - row) * lg) for lg in lg_lane]
    from_start = [jnp.exp((row + 1.0) * lg) for lg in lg_lane]

    for c in range(ts // CHUNK):
        rs = slice(c * CHUNK, (c + 1) * CHUNK)
        for jp in range(RET_HEADS // 2):
            cs = slice(jp * LANE, (jp + 1) * LANE)
            q2 = q_ref[rs, cs].astype(F32)
            k2 = k_ref[rs, cs].astype(F32)
            zero = jnp.zeros_like(q2)
            qa = jnp.where(lo_half, q2, zero)
            qb = jnp.where(lo_half, zero, q2)
            scores = _nt_dot(jnp.concatenate([qa, qb], axis=0).astype(BF16), k2.astype(BF16))
            k_end = (k2 * to_end[jp]).astype(BF16)
            st_pair = st_ref[jp * LANE:(jp + 1) * LANE, :]
            st_b = st_pair.astype(BF16)
            new_rows = []
            for hh, qh in enumerate((qa, qb)):
                h = 2 * jp + hh
                vh = v_ref[rs, h * RET_V_DIM:(h + 1) * RET_V_DIM]
                inner = _dot((scores[hh * CHUNK:(hh + 1) * CHUNK] * decay[h]).astype(BF16), vh)
                cross = _dot((qh * from_start[jp]).astype(BF16), st_b)
                kv = _tn_dot(k_end, vh)
                half = slice(hh * RET_QK_DIM, (hh + 1) * RET_QK_DIM)
                new_rows.append(st_pair[half] * math.exp(CHUNK * log_gamma[h]) + kv[half])
                yh = inner + cross
                ms = jnp.mean(yh * yh, axis=-1, keepdims=True)
                hs = slice(h * RET_V_DIM, (h + 1) * RET_V_DIM)
                y_ref[rs, hs] = (yh * lax.rsqrt(ms + EPS) * gn_ref[:, hs]
                                 * jax.nn.silu(g_ref[rs, hs].astype(F32))).astype(BF16)
            st_ref[jp * LANE:(jp + 1) * LANE, :] = jnp.concatenate(new_rows, axis=0)


def _outproj_kernel(ys_ref, rq_ref, rk_ref, rv_ref, rg_ref, rgn_ref, x_ref, w_ref, g_ref, *rest,
                    tm, dils, tiles_per_seq):
    nb = len(dils)
    o_refs, m_refs, d_refs = rest[0:nb], rest[nb:2 * nb], rest[2 * nb:3 * nb]
    xo_ref, hn_ref, il_ref, st_ref, ya_ref, yr_ref, rst_ref = rest[3 * nb:]

    @pl.when(pl.program_id(0) % tiles_per_seq == 0)
    def _():
        rst_ref[...] = jnp.zeros(rst_ref.shape, F32)

    acc = x_ref[...] + _dot(ys_ref[...], w_ref[0:SSD_WIDTH, :])
    _ret_tile(rq_ref, rk_ref, rv_ref, rg_ref, rgn_ref, yr_ref, rst_ref, ts=tm)
    maxes, sums = [], []
    for k, dil in enumerate(dils):
        if dil == 1:
            maxes.append(m_refs[k][0])
            sums.append(d_refs[k][0])
            continue
        for res in range(dil):
            rows = pl.ds(res, tm // dil, stride=dil)
            st_ref[2 * k, rows, :] = m_refs[k][res]
            st_ref[2 * k + 1, rows, :] = d_refs[k][res]
        maxes.append(st_ref[2 * k])
        sums.append(st_ref[2 * k + 1])
    m_all = functools.reduce(jnp.maximum, maxes)
    es = [jnp.exp2(m - m_all) for m in maxes]
    inv_z = 1.0 / functools.reduce(lambda a, b: a + b, [e * d for e, d in zip(es, sums)])
    wts = [e * inv_z for e in es]
    lo_half = lax.broadcasted_iota(jnp.int32, (tm, LANE), 1) < HEAD_DIM
    for c in range(ATT_WIDTH // LANE):
        cs = slice(c * LANE, (c + 1) * LANE)
        la, lb = 2 * c * STAT_LANES, (2 * c + 1) * STAT_LANES
        y_att = None
        for k, dil in enumerate(dils):
            if dil == 1:
                o = o_refs[k][0, :, cs].astype(F32)
            else:
                for res in range(dil):
                    il_ref[k, c, pl.ds(res, tm // dil, stride=dil), :] = o_refs[k][res, :, cs].astype(F32)
                o = il_ref[k, c]
            wk = jnp.where(lo_half, jnp.broadcast_to(wts[k][:, la:la + 1], (tm, LANE)),
                           jnp.broadcast_to(wts[k][:, lb:lb + 1], (tm, LANE)))
            y_att = wk * o if y_att is None else y_att + wk * o
        ya_ref[:, cs] = y_att.astype(BF16)
    acc = acc + _dot(ya_ref[...], w_ref[SSD_WIDTH:SSD_WIDTH + ATT_WIDTH, :])
    acc = acc + _dot(yr_ref[...], w_ref[SSD_WIDTH + ATT_WIDTH:MIX_WIDTH, :])
    xo_ref[...] = acc
    ms = jnp.mean(acc * acc, axis=-1, keepdims=True)
    hn_ref[...] = ((acc * lax.rsqrt(ms + EPS)) * g_ref[...]).astype(BF16)


def _outproj(y_ssd, outs, maxes, sums, qkr, proj, ret_gain, x2d, w_out, gain, seq, tm):
    t = x2d.shape[0]
    nseq = seq // tm
    dils = tuple(o.shape[1] for o in outs)
    rows = lambda width, blk=0: pl.BlockSpec((tm, width), lambda i: (i, blk))
    branch = lambda dil, width: pl.BlockSpec((None, dil, tm // dil, width), lambda i: (i // nseq, 0, i % nseq, 0))
    ret_blk = 2 * PB // RET_QK_WIDTH
    return pl.pallas_call(
        functools.partial(_outproj_kernel, tm=tm, dils=dils, tiles_per_seq=nseq),
        grid=(t // tm,),
        in_specs=[rows(SSD_WIDTH), rows(RET_QK_WIDTH, ret_blk), rows(RET_QK_WIDTH, ret_blk + 1),
                  rows(PB, BLK_RV), rows(PB, BLK_RG), pl.BlockSpec((1, RET_V_WIDTH), lambda i: (0, 0)),
                  rows(D_MODEL),
                  pl.BlockSpec((MIX_WIDTH, D_MODEL), lambda i: (0, 0), pipeline_mode=pl.Buffered(1)),
                  pl.BlockSpec((1, D_MODEL), lambda i: (0, 0))]
                 + [branch(d, ATT_WIDTH) for d in dils] + [branch(d, LANE) for d in dils] * 2,
        out_specs=[rows(D_MODEL), rows(D_MODEL)],
        out_shape=[_sds((t, D_MODEL), F32), _sds((t, D_MODEL), BF16)],
        scratch_shapes=[pltpu.VMEM((len(dils), ATT_WIDTH // LANE, tm, LANE), F32),
                        pltpu.VMEM((2 * len(dils), tm, LANE), F32),
                        pltpu.VMEM((tm, ATT_WIDTH), BF16),
                        pltpu.VMEM((tm, RET_V_WIDTH), BF16),
                        pltpu.VMEM((RET_QK_WIDTH, RET_V_DIM), F32)],
        compiler_params=_cparams(("arbitrary",)),
        name="outproj",
    )(y_ssd, qkr, qkr, proj, proj, ret_gain, x2d, w_out, gain, *outs, *maxes, *sums)


def _ffn_kernel(hn_ref, x_ref, wg_ref, wu_ref, wd_ref, *rest, n_cast):
    o_ref = rest[n_cast]

    @pl.when(pl.program_id(1) == 0)
    def _():
        o_ref[...] = x_ref[...]
        if n_cast:
            win_ref, main_ref, dt_ref = rest[0], rest[n_cast + 1], rest[n_cast + 2]
            main_ref[:, 0:DT_COL] = win_ref[:, 0:DT_COL].astype(BF16)
            main_ref[:, DT_COL:IN_MAIN] = win_ref[:, DT_COL + SSD_HEADS:IN_MAIN + SSD_HEADS].astype(BF16)
            dt_ref[...] = jnp.zeros(dt_ref.shape, BF16)
            dt_ref[:, 0:SSD_HEADS] = win_ref[:, DT_COL:DT_COL + SSD_HEADS].astype(BF16)

    h = hn_ref[...]
    act = (jax.nn.silu(_dot(h, wg_ref[...])) * _dot(h, wu_ref[...])).astype(BF16)
    o_ref[...] += _dot(act, wd_ref[...])
    for src_ref, dst_ref in zip(rest[1:n_cast], rest[n_cast + 3:]):
        dst_ref[...] = src_ref[...].astype(BF16)


def _walk_blocks(rows, cols, n_steps, n_inner):
    col_blocks = cols // LANE
    r = next(r for r in range(2 * SUBLANE, rows + 1, 2 * SUBLANE)
             if rows % r == 0 and (rows // r) * col_blocks <= n_steps)
    last = (rows // r) * col_blocks - 1

    def index(i, j):
        blk = jnp.minimum(i * n_inner + j, last)
        return blk // col_blocks, blk % col_blocks
    return pl.BlockSpec((r, LANE), index)


def _ffn(hn, x2d, w_gate, w_up, w_down, tm, tf, next_f32=None):
    t = x2d.shape[0]
    ni, nj = t // tm, D_FF // tf
    in_specs = [
        pl.BlockSpec((tm, D_MODEL), lambda i, j: (i, 0)),
        pl.BlockSpec((tm, D_MODEL), lambda i, j: (i, 0)),
        pl.BlockSpec((D_MODEL, tf), lambda i, j: (0, j)),
        pl.BlockSpec((D_MODEL, tf), lambda i, j: (0, j)),
        pl.BlockSpec((tf, D_MODEL), lambda i, j: (j, 0)),
    ]
    out_specs = [pl.BlockSpec((tm, D_MODEL), lambda i, j: (i, 0))]
    out_shape = [_sds((t, D_MODEL), F32)]
    extra = ()
    if next_f32 is not None:
        extra = tuple(next_f32)
        w_in_next = extra[0]
        rb = w_in_next.shape[0] // ni
        assert w_in_next.shape[0] % ni == 0 and rb % (2 * SUBLANE) == 0
        walks = [_walk_blocks(*w.shape, ni * nj, nj) for w in extra[1:]]
        in_specs += [pl.BlockSpec((rb, w_in_next.shape[1]), lambda i, j: (i, 0))] + walks
        out_specs += [pl.BlockSpec((rb, IN_MAIN), lambda i, j: (i, 0)), pl.BlockSpec((rb, LANE), lambda i, j: (i, 0))]
        out_specs += walks
        out_shape += [_sds((D_MODEL, IN_MAIN), BF16), _sds((D_MODEL, LANE), BF16)]
        out_shape += [_sds(w.shape, BF16) for w in extra[1:]]
    res = pl.pallas_call(
        functools.partial(_ffn_kernel, n_cast=len(extra)),
        grid=(ni, nj),
        in_specs=in_specs,
        out_specs=out_specs,
        out_shape=out_shape,
        compiler_params=_cparams(("arbitrary", "arbitrary")),
        name="ffn",
    )(hn, x2d, w_gate, w_up, w_down, *extra)
    return res[0], tuple(res[1:])


def _tiles(seq):
    return dict(
        inproj=min(512, seq),
        prep=min(512, seq),
        att=512,
        outproj=min(512, seq),
        ffn=(min(512, seq), 512),
    )


def _rope_tables(seq):
    pos = jnp.arange(seq, dtype=F32)
    inv = ROPE_THETA ** (-jnp.arange(0, HEAD_DIM, 2, dtype=F32) / HEAD_DIM)
    ang = pos[:, None] * inv[None, :]
    cos, sin = jnp.cos(ang), jnp.sin(ang)
    cosf = jnp.concatenate([cos, cos, cos, cos], axis=-1)
    sinf = jnp.concatenate([-sin, sin, -sin, sin], axis=-1)
    return cosf, sinf


def _head_sum_matrix():
    idx = np.arange(PB) // HEAD_DIM
    return jnp.asarray(idx[:, None] == idx[None, :], dtype=BF16)


def kernel(x, ln_mix, w_in, conv_w, conv_b, dt_bias, a_log, d_skip, ssd_norm, q_norm, k_norm, ret_norm, w_out,
           ln_ffn, w_gate, w_up, w_down):
    batch, seq, _ = x.shape
    depth = w_in.shape[0]
    t = batch * seq
    tiles = _tiles(seq)
    cosf, sinf = _rope_tables(seq)
    gmat = _head_sum_matrix()
    lane_pad = lambda v: jnp.pad(v.astype(F32), (0, LANE - v.shape[0]))[None, :]
    dils = tuple(dil for _, dil in DILATED_PAIRS)
    assert dils[0] == 1 and all(window // dil == ATT_BLOCK for window, dil in DILATED_PAIRS)

    x2d = x.reshape(t, D_MODEL)
    w_first = w_in[0]
    bf_weights = (
        jnp.concatenate([w_first[:, :DT_COL], w_first[:, DT_COL + SSD_HEADS:]], axis=1).astype(BF16),
        jnp.pad(w_first[:, DT_COL:DT_COL + SSD_HEADS], ((0, 0), (0, LANE - SSD_HEADS))).astype(BF16),
        w_out[0].astype(BF16), w_gate[0].astype(BF16), w_up[0].astype(BF16), w_down[0].astype(BF16))
    for i in range(depth):
        w_main, w_dt, w_out_b, w_gate_b, w_up_b, w_down_b = bf_weights
        proj, y_ssd = _inproj_ssd(x2d, ln_mix[i][None, :], w_main, w_dt, conv_w[i], conv_b[i][None, :],
                                  lane_pad(dt_bias[i]), lane_pad(a_log[i]), jnp.repeat(d_skip[i], SSD_HEAD_DIM)[None, :],
                                  ssd_norm[i][None, :], seq, tiles["inproj"])

        qkr, *qkv_dil = _prep(proj, cosf, sinf, jnp.tile(q_norm[i], ATT_HEADS)[None, :],
                              jnp.tile(k_norm[i], ATT_HEADS)[None, :], gmat, batch, seq, tiles["prep"], dils[1:])
        branch_in = [(qkr.reshape(batch, 1, seq, QKR_WIDTH), proj.reshape(batch, 1, seq, IN_MAIN), BLK_AV)]
        branch_in += [(a, a, 2) for a in qkv_dil]
        outs, maxes, sums = zip(*[_att_branch(qk_arr, v_arr, v_blk, min(tiles["att"], qk_arr.shape[2]))
                                  for qk_arr, v_arr, v_blk in branch_in])
        x2d, hn = _outproj(y_ssd, outs, maxes, sums, qkr, proj, ret_norm[i][None, :], x2d, w_out_b,
                           ln_ffn[i][None, :], seq, tiles["outproj"])
        nxt = (w_in[i + 1], w_out[i + 1], w_gate[i + 1], w_up[i + 1], w_down[i + 1]) if i + 1 < depth else None
        x2d, bf_weights = _ffn(hn, x2d, w_gate_b, w_up_b, w_down_b, *tiles["ffn"], next_f32=nxt)
    return x2d.reshape(batch, seq, D_MODEL)
```

```python
import functools
import math

import numpy as np
import jax
import jax.numpy as jnp
from jax import lax
from jax.experimental import pallas as pl
from jax.experimental.pallas import tpu as pltpu

F32 = jnp.float32
BF16 = jnp.bfloat16
NEG_INF = float("-inf")
LOG2E = math.log2(math.e)

D_MODEL = 2048
EPS = 1e-6
ROPE_THETA = 10000.0

SSD_HEADS = 16
SSD_HEAD_DIM = 64
SSD_WIDTH = SSD_HEADS * SSD_HEAD_DIM
SSD_GROUPS = 2
SSD_STATE = 128
SSD_CONV = 4
SSD_CONV_CH = SSD_WIDTH + 2 * SSD_GROUPS * SSD_STATE
CHUNK = 128

ATT_HEADS = 8
HEAD_DIM = 64
ATT_WIDTH = ATT_HEADS * HEAD_DIM
DILATED_PAIRS = ((128, 1), (512, 4), (2048, 16))
ATT_BLOCK = 128
STAT_LANES = 128 // ATT_HEADS

RET_HEADS = 4
RET_QK_DIM = 64
RET_QK_WIDTH = RET_HEADS * RET_QK_DIM
RET_V_DIM = 128
RET_V_WIDTH = RET_HEADS * RET_V_DIM

MIX_WIDTH = SSD_WIDTH + ATT_WIDTH + RET_V_WIDTH
D_FF = 5632
DT_COL = SSD_WIDTH + SSD_CONV_CH
IN_MAIN = DT_COL + 3 * ATT_WIDTH + 2 * RET_QK_WIDTH + 2 * RET_V_WIDTH

LANE = 128
SUBLANE = 8
MXU_COLS = 256
SSD_SLOTS_PER_CHUNK = 3
PB = 512
BLK_AQ, BLK_AK, BLK_AV = 5, 6, 7
BLK_RQK, BLK_RV, BLK_RG = 8, 9, 10
QKR_WIDTH = 3 * PB
VMEM_LIMIT = 56 * 1024 * 1024


def _cparams(sem):
    return pltpu.CompilerParams(dimension_semantics=sem, vmem_limit_bytes=VMEM_LIMIT)


def _sds(shape, dtype):
    return jax.ShapeDtypeStruct(shape, dtype)


def _nt_dot(a, b):
    return lax.dot_general(a, b, (((1,), (1,)), ((), ())), preferred_element_type=F32)


def _tn_dot(a, b):
    return lax.dot_general(a, b, (((0,), (0,)), ((), ())), preferred_element_type=F32)


def _dot(a, b):
    return jnp.dot(a, b, preferred_element_type=F32)


def _split3_dot(lhs_bf16, x):
    hi = x.astype(BF16)
    r = x - hi.astype(F32)
    mid = r.astype(BF16)
    lo = (r - mid.astype(F32)).astype(BF16)
    return _dot(lhs_bf16, hi) + _dot(lhs_bf16, mid) + _dot(lhs_bf16, lo)


def _inproj_ssd_kernel(x_ref, g_ref, w_ref, wdt_ref, cw_ref, cb_ref, dtb_ref, alog_ref, dsk_ref, gn_ref,
                       proj_ref, y_ref, xn_ref, src_ref, dt_ref, xpad_ref, st_ref, *, tm, tiles_per_seq):
    @pl.when(pl.program_id(1) == 0)
    def _():
        @pl.when(pl.program_id(0) % tiles_per_seq == 0)
        def _():
            xpad_ref[0:SUBLANE, :] = jnp.zeros((SUBLANE, SSD_CONV_CH), F32)
            st_ref[...] = jnp.zeros(st_ref.shape, F32)

        x = x_ref[...]
        ms = jnp.mean(x * x, axis=-1, keepdims=True)
        xn_ref[...] = ((x * lax.rsqrt(ms + EPS)) * g_ref[...]).astype(BF16)
        dt_ref[...] = _dot(xn_ref[...], wdt_ref[...])
        tile = _dot(xn_ref[...], w_ref[...]).astype(BF16)
        proj_ref[...] = tile
        src_ref[...] = tile[:, 0:DT_COL]

    @pl.when(pl.program_id(1) == 1)
    def _():
        n_pieces = w_ref.shape[1] // MXU_COLS
        n_slots = SSD_SLOTS_PER_CHUNK * (tm // CHUNK)
        slot = iter(range(n_slots))

        def matmul_pieces():
            k = next(slot)
            for p in range((n_pieces * k) // n_slots, (n_pieces * (k + 1)) // n_slots):
                ps = slice(p * MXU_COLS, (p + 1) * MXU_COLS)
                proj_ref[:, ps] = _dot(xn_ref[...], w_ref[:, ps]).astype(BF16)

        _ssd_tile(src_ref, dt_ref, cw_ref, cb_ref, dtb_ref, alog_ref, dsk_ref, gn_ref, y_ref, xpad_ref, st_ref,
                  ts=tm, between=matmul_pieces)


def _inproj_ssd(x2d, gain, w_main, w_dt, conv_w, conv_b, dt_bias, a_log, d_skip, ssd_gain, seq, tm):
    t = x2d.shape[0]
    tn = IN_MAIN // 2
    assert tn >= DT_COL and tn % MXU_COLS == 0
    full = lambda shape: pl.BlockSpec(shape, lambda i, j: (0, 0))
    return pl.pallas_call(
        functools.partial(_inproj_ssd_kernel, tm=tm, tiles_per_seq=seq // tm),
        grid=(t // tm, 2),
        in_specs=[
            pl.BlockSpec((tm, D_MODEL), lambda i, j: (i, 0)),
            full((1, D_MODEL)),
            pl.BlockSpec((D_MODEL, tn), lambda i, j: (0, j)),
            full((D_MODEL, LANE)),
            full((SSD_CONV, SSD_CONV_CH)), full((1, SSD_CONV_CH)), full((1, LANE)), full((1, LANE)),
            full((1, SSD_WIDTH)), full((1, SSD_WIDTH)),
        ],
        out_specs=[
            pl.BlockSpec((tm, tn), lambda i, j: (i, j)),
            pl.BlockSpec((tm, SSD_WIDTH), lambda i, j: (i, 0)),
        ],
        out_shape=[_sds((t, IN_MAIN), BF16), _sds((t, SSD_WIDTH), BF16)],
        scratch_shapes=[pltpu.VMEM((tm, D_MODEL), BF16), pltpu.VMEM((tm, DT_COL), BF16), pltpu.VMEM((tm, LANE), F32),
                        pltpu.VMEM((tm + SUBLANE, SSD_CONV_CH), F32), pltpu.VMEM((SSD_STATE, SSD_WIDTH), F32)],
        compiler_params=_cparams(("arbitrary", "arbitrary")),
        name="inproj_ssd",
    )(x2d, gain, w_main, w_dt, conv_w, conv_b, dt_bias, a_log, d_skip, ssd_gain)


def _prep_kernel(aq_ref, ak_ref, av_ref, rqk_ref, cos_ref, sin_ref, qg_ref, kg_ref, gm_ref,
                 tok_ref, *rest, tm, dils):
    nd = len(dils)
    dil_refs, slab_ref, stage_refs = rest[:nd], rest[nd], rest[nd + 1:]
    cosf = cos_ref[...]
    sinf = sin_ref[...]
    lane = lax.broadcasted_iota(jnp.int32, cosf.shape, 1)
    first_half = (lane & (HEAD_DIM // 2)) == 0

    def rope(xc):
        swapped = jnp.where(first_half, pltpu.roll(xc, LANE - HEAD_DIM // 2, axis=1),
                            pltpu.roll(xc, HEAD_DIM // 2, axis=1))
        return xc * cosf + swapped * sinf

    def head_rms(x_ref, gain_ref):
        x = x_ref[...].astype(F32)
        sq = x * x
        hi = sq.astype(BF16)
        lo = (sq - hi.astype(F32)).astype(BF16)
        ss = _dot(hi, gm_ref[...]) + _dot(lo, gm_ref[...])
        return (x * lax.rsqrt(ss * (1.0 / HEAD_DIM) + EPS)) * gain_ref[...]

    q = head_rms(aq_ref, qg_ref)
    k = head_rms(ak_ref, kg_ref)
    v = av_ref[...].astype(F32)
    r = rqk_ref[...].astype(F32)
    scale = HEAD_DIM ** -0.5
    n_lt = PB // LANE
    for c in range(n_lt):
        cs = slice(c * LANE, (c + 1) * LANE)
        qc = rope(q[:, cs]) * (scale * LOG2E)
        kc = rope(k[:, cs])
        rr = rope(r[:, cs])
        if c * LANE >= RET_QK_WIDTH:
            rr = rr * scale
        tok_ref[:, cs] = qc.astype(BF16)
        tok_ref[:, PB + c * LANE:PB + (c + 1) * LANE] = kc.astype(BF16)
        tok_ref[:, 2 * PB + c * LANE:2 * PB + (c + 1) * LANE] = rr.astype(BF16)
        for part, val in enumerate((qc, kc, v[:, cs])):
            slot = part * n_lt + c
            slab_ref[slot] = val
            col = slice(part * PB + c * LANE, part * PB + (c + 1) * LANE)
            parent = 1
            for lvl, (dil, d_ref) in enumerate(zip(dils, dil_refs)):
                ratio, rows = dil // parent, tm // dil
                for pres in range(parent):
                    for j in range(ratio):
                        take = pl.ds(j, rows, stride=ratio)
                        piece = slab_ref[slot, take, :] if lvl == 0 else stage_refs[lvl - 1][slot, pres, take, :]
                        res = pres + parent * j
                        d_ref[res, :, col] = piece.astype(BF16)
                        if lvl + 1 < nd:
                            stage_refs[lvl][slot, res] = piece
                parent = dil


def _prep(proj, cosf, sinf, qg, kg, gmat, batch, seq, tm, dils):
    t = proj.shape[0]
    nseq = seq // tm
    return pl.pallas_call(
        functools.partial(_prep_kernel, tm=tm, dils=dils),
        grid=(t // tm,),
        in_specs=[
            pl.BlockSpec((tm, PB), lambda i: (i, BLK_AQ)),
            pl.BlockSpec((tm, PB), lambda i: (i, BLK_AK)),
            pl.BlockSpec((tm, PB), lambda i: (i, BLK_AV)),
            pl.BlockSpec((tm, PB), lambda i: (i, BLK_RQK)),
            pl.BlockSpec((tm, LANE), lambda i: (i % nseq, 0)),
            pl.BlockSpec((tm, LANE), lambda i: (i % nseq, 0)),
            pl.BlockSpec((1, PB), lambda i: (0, 0)),
            pl.BlockSpec((1, PB), lambda i: (0, 0)),
            pl.BlockSpec((PB, PB), lambda i: (0, 0)),
        ],
        out_specs=[pl.BlockSpec((tm, QKR_WIDTH), lambda i: (i, 0))] + [
            pl.BlockSpec((None, dil, tm // dil, QKR_WIDTH), lambda i: (i // nseq, 0, i % nseq, 0)) for dil in dils],
        out_shape=[_sds((t, QKR_WIDTH), BF16)] + [_sds((batch, dil, seq // dil, QKR_WIDTH), BF16) for dil in dils],
        scratch_shapes=[pltpu.VMEM((3 * (PB // LANE), tm, LANE), F32)] + [
            pltpu.VMEM((3 * (PB // LANE), dil, tm // dil, LANE), F32) for dil in dils[:-1]],
        compiler_params=_cparams(("parallel",)),
        name="qkprep",
    )(proj, proj, proj, proj, cosf, sinf, qg, kg, gmat)


def _expand_heads(q, lo_half):
    rows = q.shape[0]
    tiles = []
    for jp in range(SSD_HEADS // 2):
        c0 = jnp.broadcast_to(q[:, 2 * jp:2 * jp + 1], (rows, LANE))
        c1 = jnp.broadcast_to(q[:, 2 * jp + 1:2 * jp + 2], (rows, LANE))
        tiles.append(jnp.where(lo_half, c0, c1))
    return jnp.concatenate(tiles, axis=1)


def _ssd_tile(src_ref, dt_ref, cw_ref, cb_ref, dtb_ref, alog_ref, dsk_ref, gn_ref, y_ref, xpad_ref, st_ref,
              *, ts, between):
    xpad_ref[SUBLANE:SUBLANE + ts, :] = src_ref[:, SSD_WIDTH:SSD_WIDTH + SSD_CONV_CH].astype(F32)

    row = lax.broadcasted_iota(jnp.int32, (CHUNK, CHUNK), 0)
    col = lax.broadcasted_iota(jnp.int32, (CHUNK, CHUNK), 1)
    causal = row >= col
    ltri = jnp.where(causal, 1.0, 0.0).astype(BF16)
    lo_half = col < HEAD_DIM
    a_neg = -jnp.exp(alog_ref[...])
    gw = SSD_WIDTH // SSD_GROUPS

    for c in range(ts // CHUNK):
        r0 = c * CHUNK
        acc = cb_ref[...]
        for i in range(SSD_CONV):
            off = r0 + SUBLANE - (SSD_CONV - 1) + i
            acc = acc + cw_ref[i:i + 1, :] * xpad_ref[off:off + CHUNK, :]
        xc = jax.nn.silu(acc)
        xs = xc[:, 0:SSD_WIDTH]
        bmat = xc[:, SSD_WIDTH:SSD_WIDTH + SSD_GROUPS * SSD_STATE].astype(BF16)
        cmat = xc[:, SSD_WIDTH + SSD_GROUPS * SSD_STATE:SSD_CONV_CH].astype(BF16)

        dt = jax.nn.softplus(dt_ref[r0:r0 + CHUNK, :] + dtb_ref[...])
        acum = _split3_dot(ltri, dt * a_neg)
        acum_t = acum.T
        dt_t = dt.T
        a_last = acum[CHUNK - 1:CHUNK, :]
        w_end = _expand_heads(dt * jnp.exp(a_last - acum), lo_half)
        ea = _expand_heads(jnp.exp(acum), lo_half)
        xw = (xs * w_end).astype(BF16)
        xs_b = xs.astype(BF16)

        y_tiles = []
        between()
        for g in range(SSD_GROUPS):
            bg = bmat[:, g * SSD_STATE:(g + 1) * SSD_STATE]
            cg = cmat[:, g * SSD_STATE:(g + 1) * SSD_STATE]
            cbm = _nt_dot(cg, bg)
            st_prev = st_ref[:, g * gw:(g + 1) * gw]
            y_off = _dot(cg, st_prev.astype(BF16)) * ea[:, g * gw:(g + 1) * gw]
            chunk_state = _tn_dot(bg, xw[:, g * gw:(g + 1) * gw])
            st_ref[:, g * gw:(g + 1) * gw] = st_prev * ea[CHUNK - 1:CHUNK, g * gw:(g + 1) * gw] + chunk_state
            for hp in range(SSD_HEADS // SSD_GROUPS // 2):
                h0 = g * (SSD_HEADS // SSD_GROUPS) + 2 * hp
                mats = []
                for h in (h0, h0 + 1):
                    seg = jnp.broadcast_to(acum[:, h:h + 1], (CHUNK, CHUNK)) - acum_t[h:h + 1, :]
                    dec = jnp.exp(jnp.where(causal, seg, NEG_INF))
                    mats.append((cbm * dec * dt_t[h:h + 1, :]).astype(BF16))
                yd = _dot(jnp.concatenate(mats, axis=0), xs_b[:, h0 * HEAD_DIM:h0 * HEAD_DIM + LANE])
                lt = slice(hp * LANE, (hp + 1) * LANE)
                y_tiles.append(jnp.where(lo_half, yd[0:CHUNK], yd[CHUNK:2 * CHUNK]) + y_off[:, lt])
        between()
        y = jnp.concatenate(y_tiles, axis=1) + xs * dsk_ref[...]
        y = y * jax.nn.silu(src_ref[r0:r0 + CHUNK, 0:SSD_WIDTH].astype(F32))
        for g in range(SSD_GROUPS):
            yg = y[:, g * gw:(g + 1) * gw]
            ms = jnp.mean(yg * yg, axis=-1, keepdims=True)
            y_ref[r0:r0 + CHUNK, g * gw:(g + 1) * gw] = (
                yg * lax.rsqrt(ms + EPS) * gn_ref[:, g * gw:(g + 1) * gw]).astype(BF16)
        between()

    xpad_ref[0:SUBLANE, :] = xpad_ref[ts:ts + SUBLANE, :]


def _att_kernel(q_ref, kp_ref, kc_ref, vp_ref, vc_ref, o_ref, m_ref, d_ref, *, tq):
    n = pl.program_id(2)
    blk = ATT_BLOCK
    iq = lax.broadcasted_iota(jnp.int32, (2 * blk, 2 * blk), 0) & (blk - 1)
    ik = lax.broadcasted_iota(jnp.int32, (2 * blk, 2 * blk), 1)
    delta = ik - iq
    band = (delta >= 0) & (delta <= blk)
    bias = jnp.where(band, 0.0, NEG_INF)
    before_start = jnp.where(n > 0, 0.0, NEG_INF)
    bias_first = jnp.where(ik >= blk, bias, jnp.minimum(bias, before_start))
    lane = lax.broadcasted_iota(jnp.int32, (blk, LANE), 1)
    lo_half = lane < HEAD_DIM
    head_of_lane = lane // STAT_LANES

    for i in range(tq // blk):
        rs = slice(i * blk, (i + 1) * blk)
        m_c = jnp.zeros((blk, LANE), F32)
        d_c = jnp.zeros((blk, LANE), F32)
        for j in range(ATT_WIDTH // LANE):
            cs = slice(j * LANE, (j + 1) * LANE)
            q2 = q_ref[rs, cs]
            zero = jnp.zeros_like(q2)
            qq = jnp.concatenate([jnp.where(lo_half, q2, zero), jnp.where(lo_half, zero, q2)], axis=0)
            if i == 0:
                k2 = jnp.concatenate([kp_ref[:, cs], kc_ref[0:blk, cs]], axis=0)
                v2 = jnp.concatenate([vp_ref[:, cs], vc_ref[0:blk, cs]], axis=0)
            else:
                k2 = kc_ref[(i - 1) * blk:(i + 1) * blk, cs]
                v2 = vc_ref[(i - 1) * blk:(i + 1) * blk, cs]
            s = _nt_dot(qq, k2) + (bias_first if i == 0 else bias)
            m = jnp.max(s, axis=-1, keepdims=True)
            p = jnp.exp2(s - m)
            den = jnp.sum(p, axis=-1, keepdims=True)
            pv = _dot(p.astype(BF16), v2)
            o_ref[rs, cs] = jnp.where(lo_half, pv[0:blk], pv[blk:2 * blk]).astype(BF16)
            for hh in range(2):
                sel = head_of_lane == 2 * j + hh
                m_c = jnp.where(sel, m[hh * blk:(hh + 1) * blk], m_c)
                d_c = jnp.where(sel, den[hh * blk:(hh + 1) * blk], d_c)
        m_ref[rs, :] = m_c
        d_ref[rs, :] = d_c


def _att_branch(qk_arr, v_arr, v_blk, tq):
    batch, dil, sub, _ = qk_arr.shape
    kb = tq // ATT_BLOCK
    tile = lambda blk: pl.BlockSpec((None, None, tq, PB), lambda b, r, n: (b, r, n, blk))
    stat = pl.BlockSpec((None, None, tq, LANE), lambda b, r, n: (b, r, n, 0))
    prev = lambda blk: pl.BlockSpec((None, None, ATT_BLOCK, PB),
                                    lambda b, r, n: (b, r, jnp.maximum(n * kb - 1, 0), blk))
    return pl.pallas_call(
        functools.partial(_att_kernel, tq=tq),
        grid=(batch, dil, sub // tq),
        in_specs=[tile(0), prev(1), tile(1), prev(v_blk), tile(v_blk)],
        out_specs=[tile(0), stat, stat],
        out_shape=[_sds((batch, dil, sub, ATT_WIDTH), BF16), _sds((batch, dil, sub, LANE), F32),
                   _sds((batch, dil, sub, LANE), F32)],
        compiler_params=_cparams(("parallel", "parallel", "arbitrary")),
        name=f"att_d{dil}",
    )(qk_arr, qk_arr, qk_arr, v_arr, v_arr)


def _ret_tile(q_ref, k_ref, v_ref, g_ref, gn_ref, y_ref, st_ref, *, ts):
    row_i = lax.broadcasted_iota(jnp.int32, (CHUNK, LANE), 0)
    col_i = lax.broadcasted_iota(jnp.int32, (CHUNK, LANE), 1)
    row = row_i.astype(F32)
    rel = (row_i - col_i).astype(F32)
    lo_half = col_i < RET_QK_DIM
    log_gamma = [math.log1p(-(2.0 ** (-5.0 - h))) for h in range(RET_HEADS)]
    decay = [jnp.where(rel >= 0, jnp.exp(jnp.maximum(rel, 0.0) * lg), 0.0) for lg in log_gamma]
    lg_lane = [jnp.where(lo_half, log_gamma[2 * jp], log_gamma[2 * jp + 1]) for jp in range(RET_HEADS // 2)]
    to_end = [jnp.exp((CHUNK - 1.0 - row) * lg) for lg in lg_lane]
    from_start = [jnp.exp((row + 1.0) * lg) for lg in lg_lane]

    for c in range(ts // CHUNK):
        rs = slice(c * CHUNK, (c + 1) * CHUNK)
        for jp in range(RET_HEADS // 2):
            cs = slice(jp * LANE, (jp + 1) * LANE)
            q2 = q_ref[rs, cs].astype(F32)
            k2 = k_ref[rs, cs].astype(F32)
            zero = jnp.zeros_like(q2)
            qa = jnp.where(lo_half, q2, zero)
            qb = jnp.where(lo_half, zero, q2)
            scores = _nt_dot(jnp.concatenate([qa, qb], axis=0).astype(BF16), k2.astype(BF16))
            k_end = (k2 * to_end[jp]).astype(BF16)
            st_pair = st_ref[jp * LANE:(jp + 1) * LANE, :]
            st_b = st_pair.astype(BF16)
            new_rows = []
            for hh, qh in enumerate((qa, qb)):
                h = 2 * jp + hh
                vh = v_ref[rs, h * RET_V_DIM:(h + 1) * RET_V_DIM]
                inner = _dot((scores[hh * CHUNK:(hh + 1) * CHUNK] * decay[h]).astype(BF16), vh)
                cross = _dot((qh * from_start[jp]).astype(BF16), st_b)
                kv = _tn_dot(k_end, vh)
                half = slice(hh * RET_QK_DIM, (hh + 1) * RET_QK_DIM)
                new_rows.append(st_pair[half] * math.exp(CHUNK * log_gamma[h]) + kv[half])
                yh = inner + cross
                ms = jnp.mean(yh * yh, axis=-1, keepdims=True)
                hs = slice(h * RET_V_DIM, (h + 1) * RET_V_DIM)
                y_ref[rs, hs] = (yh * lax.rsqrt(ms + EPS) * gn_ref[:, hs]
                                 * jax.nn.silu(g_ref[rs, hs].astype(F32))).astype(BF16)
            st_ref[jp * LANE:(jp + 1) * LANE, :] = jnp.concatenate(new_rows, axis=0)


def _outproj_kernel(ys_ref, rq_ref, rk_ref, rv_ref, rg_ref, rgn_ref, x_ref, w_ref, g_ref, *rest,
                    tm, dils, tiles_per_seq):
    nb = len(dils)
    o_refs, m_refs, d_refs = rest[0:nb], rest[nb:2 * nb], rest[2 * nb:3 * nb]
    xo_ref, hn_ref, il_ref, st_ref, ya_ref, yr_ref, rst_ref = rest[3 * nb:]

    @pl.when(pl.program_id(0) % tiles_per_seq == 0)
    def _():
        rst_ref[...] = jnp.zeros(rst_ref.shape, F32)

    acc = x_ref[...] + _dot(ys_ref[...], w_ref[0:SSD_WIDTH, :])
    _ret_tile(rq_ref, rk_ref, rv_ref, rg_ref, rgn_ref, yr_ref, rst_ref, ts=tm)
    maxes, sums = [], []
    for k, dil in enumerate(dils):
        if dil == 1:
            maxes.append(m_refs[k][0])
            sums.append(d_refs[k][0])
            continue
        for res in range(dil):
            rows = pl.ds(res, tm // dil, stride=dil)
            st_ref[2 * k, rows, :] = m_refs[k][res]
            st_ref[2 * k + 1, rows, :] = d_refs[k][res]
        maxes.append(st_ref[2 * k])
        sums.append(st_ref[2 * k + 1])
    m_all = functools.reduce(jnp.maximum, maxes)
    es = [jnp.exp2(m - m_all) for m in maxes]
    inv_z = 1.0 / functools.reduce(lambda a, b: a + b, [e * d for e, d in zip(es, sums)])
    wts = [e * inv_z for e in es]
    lo_half = lax.broadcasted_iota(jnp.int32, (tm, LANE), 1) < HEAD_DIM
    for c in range(ATT_WIDTH // LANE):
        cs = slice(c * LANE, (c + 1) * LANE)
        la, lb = 2 * c * STAT_LANES, (2 * c + 1) * STAT_LANES
        y_att = None
        for k, dil in enumerate(dils):
            if dil == 1:
                o = o_refs[k][0, :, cs].astype(F32)
            else:
                for res in range(dil):
                    il_ref[k, c, pl.ds(res, tm // dil, stride=dil), :] = o_refs[k][res, :, cs].astype(F32)
                o = il_ref[k, c]
            wk = jnp.where(lo_half, jnp.broadcast_to(wts[k][:, la:la + 1], (tm, LANE)),
                           jnp.broadcast_to(wts[k][:, lb:lb + 1], (tm, LANE)))
            y_att = wk * o if y_att is None else y_att + wk * o
        ya_ref[:, cs] = y_att.astype(BF16)
    acc = acc + _dot(ya_ref[...], w_ref[SSD_WIDTH:SSD_WIDTH + ATT_WIDTH, :])
    acc = acc + _dot(yr_ref[...], w_ref[SSD_WIDTH + ATT_WIDTH:MIX_WIDTH, :])
    xo_ref[...] = acc
    ms = jnp.mean(acc * acc, axis=-1, keepdims=True)
    hn_ref[...] = ((acc * lax.rsqrt(ms + EPS)) * g_ref[...]).astype(BF16)


def _outproj(y_ssd, outs, maxes, sums, qkr, proj, ret_gain, x2d, w_out, gain, seq, tm):
    t = x2d.shape[0]
    nseq = seq // tm
    dils = tuple(o.shape[1] for o in outs)
    rows = lambda width, blk=0: pl.BlockSpec((tm, width), lambda i: (i, blk))
    branch = lambda dil, width: pl.BlockSpec((None, dil, tm // dil, width), lambda i: (i // nseq, 0, i % nseq, 0))
    ret_blk = 2 * PB // RET_QK_WIDTH
    return pl.pallas_call(
        functools.partial(_outproj_kernel, tm=tm, dils=dils, tiles_per_seq=nseq),
        grid=(t // tm,),
        in_specs=[rows(SSD_WIDTH), rows(RET_QK_WIDTH, ret_blk), rows(RET_QK_WIDTH, ret_blk + 1),
                  rows(PB, BLK_RV), rows(PB, BLK_RG), pl.BlockSpec((1, RET_V_WIDTH), lambda i: (0, 0)),
                  rows(D_MODEL),
                  pl.BlockSpec((MIX_WIDTH, D_MODEL), lambda i: (0, 0), pipeline_mode=pl.Buffered(1)),
                  pl.BlockSpec((1, D_MODEL), lambda i: (0, 0))]
                 + [branch(d, ATT_WIDTH) for d in dils] + [branch(d, LANE) for d in dils] * 2,
        out_specs=[rows(D_MODEL), rows(D_MODEL)],
        out_shape=[_sds((t, D_MODEL), F32), _sds((t, D_MODEL), BF16)],
        scratch_shapes=[pltpu.VMEM((len(dils), ATT_WIDTH // LANE, tm, LANE), F32),
                        pltpu.VMEM((2 * len(dils), tm, LANE), F32),
                        pltpu.VMEM((tm, ATT_WIDTH), BF16),
                        pltpu.VMEM((tm, RET_V_WIDTH), BF16),
                        pltpu.VMEM((RET_QK_WIDTH, RET_V_DIM), F32)],
        compiler_params=_cparams(("arbitrary",)),
        name="outproj",
    )(y_ssd, qkr, qkr, proj, proj, ret_gain, x2d, w_out, gain, *outs, *maxes, *sums)


def _ffn_kernel(hn_ref, x_ref, wg_ref, wu_ref, wd_ref, o_ref):
    @pl.when(pl.program_id(1) == 0)
    def _():
        o_ref[...] = x_ref[...]

    h = hn_ref[...]
    act = (jax.nn.silu(_dot(h, wg_ref[...])) * _dot(h, wu_ref[...])).astype(BF16)
    o_ref[...] += _dot(act, wd_ref[...])


def _ffn(hn, x2d, w_gate, w_up, w_down, tm, tf):
    t = x2d.shape[0]
    return pl.pallas_call(
        _ffn_kernel,
        grid=(t // tm, D_FF // tf),
        in_specs=[
            pl.BlockSpec((tm, D_MODEL), lambda i, j: (i, 0)),
            pl.BlockSpec((tm, D_MODEL), lambda i, j: (i, 0)),
            pl.BlockSpec((D_MODEL, tf), lambda i, j: (0, j)),
            pl.BlockSpec((D_MODEL, tf), lambda i, j: (0, j)),
            pl.BlockSpec((tf, D_MODEL), lambda i, j: (j, 0)),
        ],
        out_specs=pl.BlockSpec((tm, D_MODEL), lambda i, j: (i, 0)),
        out_shape=_sds((t, D_MODEL), F32),
        compiler_params=_cparams(("parallel", "arbitrary")),
        name="ffn",
    )(hn, x2d, w_gate, w_up, w_down)


def _tiles(seq):
    return dict(
        inproj=min(512, seq),
        prep=min(512, seq),
        att=512,
        outproj=min(512, seq),
        ffn=(min(512, seq), 512),
    )


def _rope_tables(seq):
    pos = jnp.arange(seq, dtype=F32)
    inv = ROPE_THETA ** (-jnp.arange(0, HEAD_DIM, 2, dtype=F32) / HEAD_DIM)
    ang = pos[:, None] * inv[None, :]
    cos, sin = jnp.cos(ang), jnp.sin(ang)
    cosf = jnp.concatenate([cos, cos, cos, cos], axis=-1)
    sinf = jnp.concatenate([-sin, sin, -sin, sin], axis=-1)
    return cosf, sinf


def _head_sum_matrix():
    idx = np.arange(PB) // HEAD_DIM
    return jnp.asarray(idx[:, None] == idx[None, :], dtype=BF16)


def kernel(x, ln_mix, w_in, conv_w, conv_b, dt_bias, a_log, d_skip, ssd_norm, q_norm, k_norm, ret_norm, w_out,
           ln_ffn, w_gate, w_up, w_down):
    batch, seq, _ = x.shape
    depth = w_in.shape[0]
    t = batch * seq
    tiles = _tiles(seq)
    cosf, sinf = _rope_tables(seq)
    gmat = _head_sum_matrix()
    lane_pad = lambda v: jnp.pad(v.astype(F32), (0, LANE - v.shape[0]))[None, :]
    dils = tuple(dil for _, dil in DILATED_PAIRS)
    assert dils[0] == 1 and all(window // dil == ATT_BLOCK for window, dil in DILATED_PAIRS)

    x2d = x.reshape(t, D_MODEL)
    for i in range(depth):
        w_i = w_in[i]
        w_main = jnp.concatenate([w_i[:, :DT_COL], w_i[:, DT_COL + SSD_HEADS:]], axis=1).astype(BF16)
        w_dt = jnp.pad(w_i[:, DT_COL:DT_COL + SSD_HEADS], ((0, 0), (0, LANE - SSD_HEADS))).astype(BF16)
        proj, y_ssd = _inproj_ssd(x2d, ln_mix[i][None, :], w_main, w_dt, conv_w[i], conv_b[i][None, :],
                                  lane_pad(dt_bias[i]), lane_pad(a_log[i]), jnp.repeat(d_skip[i], SSD_HEAD_DIM)[None, :],
                                  ssd_norm[i][None, :], seq, tiles["inproj"])

        qkr, *qkv_dil = _prep(proj, cosf, sinf, jnp.tile(q_norm[i], ATT_HEADS)[None, :],
                              jnp.tile(k_norm[i], ATT_HEADS)[None, :], gmat, batch, seq, tiles["prep"], dils[1:])
        branch_in = [(qkr.reshape(batch, 1, seq, QKR_WIDTH), proj.reshape(batch, 1, seq, IN_MAIN), BLK_AV)]
        branch_in += [(a, a, 2) for a in qkv_dil]
        outs, maxes, sums = zip(*[_att_branch(qk_arr, v_arr, v_blk, min(tiles["att"], qk_arr.shape[2]))
                                  for qk_arr, v_arr, v_blk in branch_in])
        x2d, hn = _outproj(y_ssd, outs, maxes, sums, qkr, proj, ret_norm[i][None, :], x2d, w_out[i].astype(BF16),
                           ln_ffn[i][None, :], seq, tiles["outproj"])
        x2d = _ffn(hn, x2d, w_gate[i].astype(BF16), w_up[i].astype(BF16), w_down[i].astype(BF16), *tiles["ffn"])
    return x2d.reshape(batch, seq, D_MODEL)
```

```python
import functools
import math

import numpy as np
import jax
import jax.numpy as jnp
from jax import lax
from jax.experimental import pallas as pl
from jax.experimental.pallas import tpu as pltpu

F32 = jnp.float32
BF16 = jnp.bfloat16
NEG_INF = float("-inf")
LOG2E = math.log2(math.e)

D_MODEL = 2048
EPS = 1e-6
ROPE_THETA = 10000.0

SSD_HEADS = 16
SSD_HEAD_DIM = 64
SSD_WIDTH = SSD_HEADS * SSD_HEAD_DIM
SSD_GROUPS = 2
SSD_STATE = 128
SSD_CONV = 4
SSD_CONV_CH = SSD_WIDTH + 2 * SSD_GROUPS * SSD_STATE
CHUNK = 128

ATT_HEADS = 8
HEAD_DIM = 64
ATT_WIDTH = ATT_HEADS * HEAD_DIM
DILATED_PAIRS = ((128, 1), (512, 4), (2048, 16))
ATT_BLOCK = 128
STAT_LANES = 128 // ATT_HEADS

RET_HEADS = 4
RET_QK_DIM = 64
RET_QK_WIDTH = RET_HEADS * RET_QK_DIM
RET_V_DIM = 128
RET_V_WIDTH = RET_HEADS * RET_V_DIM

MIX_WIDTH = SSD_WIDTH + ATT_WIDTH + RET_V_WIDTH
D_FF = 5632
DT_COL = SSD_WIDTH + SSD_CONV_CH
IN_MAIN = DT_COL + 3 * ATT_WIDTH + 2 * RET_QK_WIDTH + 2 * RET_V_WIDTH

LANE = 128
SUBLANE = 8
MXU_COLS = 256
SSD_SLOTS_PER_CHUNK = 3 + SSD_HEADS // 2
PB = 512
BLK_AQ, BLK_AK, BLK_AV = 5, 6, 7
BLK_RQK, BLK_RV, BLK_RG = 8, 9, 10
QKR_WIDTH = 3 * PB
VMEM_LIMIT = 56 * 1024 * 1024


def _cparams(sem):
    return pltpu.CompilerParams(dimension_semantics=sem, vmem_limit_bytes=VMEM_LIMIT)


def _sds(shape, dtype):
    return jax.ShapeDtypeStruct(shape, dtype)


def _nt_dot(a, b):
    return lax.dot_general(a, b, (((1,), (1,)), ((), ())), preferred_element_type=F32)


def _tn_dot(a, b):
    return lax.dot_general(a, b, (((0,), (0,)), ((), ())), preferred_element_type=F32)


def _dot(a, b):
    return jnp.dot(a, b, preferred_element_type=F32)


def _split3_dot(lhs_bf16, x):
    hi = x.astype(BF16)
    r = x - hi.astype(F32)
    mid = r.astype(BF16)
    lo = (r - mid.astype(F32)).astype(BF16)
    return _dot(lhs_bf16, hi) + _dot(lhs_bf16, mid) + _dot(lhs_bf16, lo)


def _inproj_ssd_kernel(x_ref, g_ref, w_ref, wdt_ref, cw_ref, cb_ref, dtb_ref, alog_ref, dsk_ref, gn_ref,
                       proj_ref, y_ref, xn_ref, src_ref, dt_ref, xpad_ref, st_ref, *, tm, tiles_per_seq):
    @pl.when(pl.program_id(1) == 0)
    def _():
        @pl.when(pl.program_id(0) % tiles_per_seq == 0)
        def _():
            xpad_ref[0:SUBLANE, :] = jnp.zeros((SUBLANE, SSD_CONV_CH), F32)
            st_ref[...] = jnp.zeros(st_ref.shape, F32)

        x = x_ref[...]
        ms = jnp.mean(x * x, axis=-1, keepdims=True)
        xn_ref[...] = ((x * lax.rsqrt(ms + EPS)) * g_ref[...]).astype(BF16)
        dt_ref[...] = _dot(xn_ref[...], wdt_ref[...])
        tile = _dot(xn_ref[...], w_ref[...]).astype(BF16)
        proj_ref[...] = tile
        src_ref[...] = tile[:, 0:DT_COL]

    @pl.when(pl.program_id(1) == 1)
    def _():
        n_pieces = w_ref.shape[1] // MXU_COLS
        n_slots = SSD_SLOTS_PER_CHUNK * (tm // CHUNK)
        slot = iter(range(n_slots))

        def matmul_pieces():
            k = next(slot)
            for p in range((n_pieces * k) // n_slots, (n_pieces * (k + 1)) // n_slots):
                ps = slice(p * MXU_COLS, (p + 1) * MXU_COLS)
                proj_ref[:, ps] = _dot(xn_ref[...], w_ref[:, ps]).astype(BF16)

        _ssd_tile(src_ref, dt_ref, cw_ref, cb_ref, dtb_ref, alog_ref, dsk_ref, gn_ref, y_ref, xpad_ref, st_ref,
                  ts=tm, between=matmul_pieces)


def _inproj_ssd(x2d, gain, w_main, w_dt, conv_w, conv_b, dt_bias, a_log, d_skip, ssd_gain, seq, tm):
    t = x2d.shape[0]
    tn = IN_MAIN // 2
    assert tn >= DT_COL and tn % MXU_COLS == 0
    full = lambda shape: pl.BlockSpec(shape, lambda i, j: (0, 0))
    return pl.pallas_call(
        functools.partial(_inproj_ssd_kernel, tm=tm, tiles_per_seq=seq // tm),
        grid=(t // tm, 2),
        in_specs=[
            pl.BlockSpec((tm, D_MODEL), lambda i, j: (i, 0)),
            full((1, D_MODEL)),
            pl.BlockSpec((D_MODEL, tn), lambda i, j: (0, j)),
            full((D_MODEL, LANE)),
            full((SSD_CONV, SSD_CONV_CH)), full((1, SSD_CONV_CH)), full((1, LANE)), full((1, LANE)),
            full((1, SSD_WIDTH)), full((1, SSD_WIDTH)),
        ],
        out_specs=[
            pl.BlockSpec((tm, tn), lambda i, j: (i, j)),
            pl.BlockSpec((tm, SSD_WIDTH), lambda i, j: (i, 0)),
        ],
        out_shape=[_sds((t, IN_MAIN), BF16), _sds((t, SSD_WIDTH), BF16)],
        scratch_shapes=[pltpu.VMEM((tm, D_MODEL), BF16), pltpu.VMEM((tm, DT_COL), BF16), pltpu.VMEM((tm, LANE), F32),
                        pltpu.VMEM((tm + SUBLANE, SSD_CONV_CH), F32), pltpu.VMEM((SSD_STATE, SSD_WIDTH), F32)],
        compiler_params=_cparams(("arbitrary", "arbitrary")),
        name="inproj_ssd",
    )(x2d, gain, w_main, w_dt, conv_w, conv_b, dt_bias, a_log, d_skip, ssd_gain)


def _prep_kernel(aq_ref, ak_ref, av_ref, rqk_ref, cos_ref, sin_ref, qg_ref, kg_ref, gm_ref,
                 tok_ref, *rest, tm, dils):
    nd = len(dils)
    dil_refs, slab_ref, stage_refs = rest[:nd], rest[nd], rest[nd + 1:]
    cosf = cos_ref[...]
    sinf = sin_ref[...]
    lane = lax.broadcasted_iota(jnp.int32, cosf.shape, 1)
    first_half = (lane & (HEAD_DIM // 2)) == 0

    def rope(xc):
        swapped = jnp.where(first_half, pltpu.roll(xc, LANE - HEAD_DIM // 2, axis=1),
                            pltpu.roll(xc, HEAD_DIM // 2, axis=1))
        return xc * cosf + swapped * sinf

    def head_rms(x_ref, gain_ref):
        x = x_ref[...].astype(F32)
        sq = x * x
        hi = sq.astype(BF16)
        lo = (sq - hi.astype(F32)).astype(BF16)
        ss = _dot(hi, gm_ref[...]) + _dot(lo, gm_ref[...])
        return (x * lax.rsqrt(ss * (1.0 / HEAD_DIM) + EPS)) * gain_ref[...]

    q = head_rms(aq_ref, qg_ref)
    k = head_rms(ak_ref, kg_ref)
    v = av_ref[...].astype(F32)
    r = rqk_ref[...].astype(F32)
    scale = HEAD_DIM ** -0.5
    n_lt = PB // LANE
    for c in range(n_lt):
        cs = slice(c * LANE, (c + 1) * LANE)
        qc = rope(q[:, cs]) * (scale * LOG2E)
        kc = rope(k[:, cs])
        rr = rope(r[:, cs])
        if c * LANE >= RET_QK_WIDTH:
            rr = rr * scale
        tok_ref[:, cs] = qc.astype(BF16)
        tok_ref[:, PB + c * LANE:PB + (c + 1) * LANE] = kc.astype(BF16)
        tok_ref[:, 2 * PB + c * LANE:2 * PB + (c + 1) * LANE] = rr.astype(BF16)
        for part, val in enumerate((qc, kc, v[:, cs])):
            slot = part * n_lt + c
            slab_ref[slot] = val
            col = slice(part * PB + c * LANE, part * PB + (c + 1) * LANE)
            parent = 1
            for lvl, (dil, d_ref) in enumerate(zip(dils, dil_refs)):
                ratio, rows = dil // parent, tm // dil
                for pres in range(parent):
                    for j in range(ratio):
                        take = pl.ds(j, rows, stride=ratio)
                        piece = slab_ref[slot, take, :] if lvl == 0 else stage_refs[lvl - 1][slot, pres, take, :]
                        res = pres + parent * j
                        d_ref[res, :, col] = piece.astype(BF16)
                        if lvl + 1 < nd:
                            stage_refs[lvl][slot, res] = piece
                parent = dil


def _prep(proj, cosf, sinf, qg, kg, gmat, batch, seq, tm, dils):
    t = proj.shape[0]
    nseq = seq // tm
    return pl.pallas_call(
        functools.partial(_prep_kernel, tm=tm, dils=dils),
        grid=(t // tm,),
        in_specs=[
            pl.BlockSpec((tm, PB), lambda i: (i, BLK_AQ)),
            pl.BlockSpec((tm, PB), lambda i: (i, BLK_AK)),
            pl.BlockSpec((tm, PB), lambda i: (i, BLK_AV)),
            pl.BlockSpec((tm, PB), lambda i: (i, BLK_RQK)),
            pl.BlockSpec((tm, LANE), lambda i: (i % nseq, 0)),
            pl.BlockSpec((tm, LANE), lambda i: (i % nseq, 0)),
            pl.BlockSpec((1, PB), lambda i: (0, 0)),
            pl.BlockSpec((1, PB), lambda i: (0, 0)),
            pl.BlockSpec((PB, PB), lambda i: (0, 0)),
        ],
        out_specs=[pl.BlockSpec((tm, QKR_WIDTH), lambda i: (i, 0))] + [
            pl.BlockSpec((None, dil, tm // dil, QKR_WIDTH), lambda i: (i // nseq, 0, i % nseq, 0)) for dil in dils],
        out_shape=[_sds((t, QKR_WIDTH), BF16)] + [_sds((batch, dil, seq // dil, QKR_WIDTH), BF16) for dil in dils],
        scratch_shapes=[pltpu.VMEM((3 * (PB // LANE), tm, LANE), F32)] + [
            pltpu.VMEM((3 * (PB // LANE), dil, tm // dil, LANE), F32) for dil in dils[:-1]],
        compiler_params=_cparams(("parallel",)),
        name="qkprep",
    )(proj, proj, proj, proj, cosf, sinf, qg, kg, gmat)


def _expand_heads(q, lo_half):
    rows = q.shape[0]
    tiles = []
    for jp in range(SSD_HEADS // 2):
        c0 = jnp.broadcast_to(q[:, 2 * jp:2 * jp + 1], (rows, LANE))
        c1 = jnp.broadcast_to(q[:, 2 * jp + 1:2 * jp + 2], (rows, LANE))
        tiles.append(jnp.where(lo_half, c0, c1))
    return jnp.concatenate(tiles, axis=1)


def _ssd_tile(src_ref, dt_ref, cw_ref, cb_ref, dtb_ref, alog_ref, dsk_ref, gn_ref, y_ref, xpad_ref, st_ref,
              *, ts, between):
    xpad_ref[SUBLANE:SUBLANE + ts, :] = src_ref[:, SSD_WIDTH:SSD_WIDTH + SSD_CONV_CH].astype(F32)

    row = lax.broadcasted_iota(jnp.int32, (CHUNK, CHUNK), 0)
    col = lax.broadcasted_iota(jnp.int32, (CHUNK, CHUNK), 1)
    causal = row >= col
    ltri = jnp.where(causal, 1.0, 0.0).astype(BF16)
    lo_half = col < HEAD_DIM
    a_neg = -jnp.exp(alog_ref[...])
    gw = SSD_WIDTH // SSD_GROUPS

    for c in range(ts // CHUNK):
        r0 = c * CHUNK
        acc = cb_ref[...]
        for i in range(SSD_CONV):
            off = r0 + SUBLANE - (SSD_CONV - 1) + i
            acc = acc + cw_ref[i:i + 1, :] * xpad_ref[off:off + CHUNK, :]
        xc = jax.nn.silu(acc)
        xs = xc[:, 0:SSD_WIDTH]
        bmat = xc[:, SSD_WIDTH:SSD_WIDTH + SSD_GROUPS * SSD_STATE].astype(BF16)
        cmat = xc[:, SSD_WIDTH + SSD_GROUPS * SSD_STATE:SSD_CONV_CH].astype(BF16)

        dt = jax.nn.softplus(dt_ref[r0:r0 + CHUNK, :] + dtb_ref[...])
        acum = _split3_dot(ltri, dt * a_neg)
        acum_t = acum.T
        dt_t = dt.T
        a_last = acum[CHUNK - 1:CHUNK, :]
        w_end = _expand_heads(dt * jnp.exp(a_last - acum), lo_half)
        ea = _expand_heads(jnp.exp(acum), lo_half)
        xw = (xs * w_end).astype(BF16)
        xs_b = xs.astype(BF16)

        y_tiles = []
        between()
        for g in range(SSD_GROUPS):
            bg = bmat[:, g * SSD_STATE:(g + 1) * SSD_STATE]
            cg = cmat[:, g * SSD_STATE:(g + 1) * SSD_STATE]
            cbm = _nt_dot(cg, bg)
            st_prev = st_ref[:, g * gw:(g + 1) * gw]
            y_off = _dot(cg, st_prev.astype(BF16)) * ea[:, g * gw:(g + 1) * gw]
            chunk_state = _tn_dot(bg, xw[:, g * gw:(g + 1) * gw])
            st_ref[:, g * gw:(g + 1) * gw] = st_prev * ea[CHUNK - 1:CHUNK, g * gw:(g + 1) * gw] + chunk_state
            for hp in range(SSD_HEADS // SSD_GROUPS // 2):
                h0 = g * (SSD_HEADS // SSD_GROUPS) + 2 * hp
                mats = []
                for h in (h0, h0 + 1):
                    seg = jnp.broadcast_to(acum[:, h:h + 1], (CHUNK, CHUNK)) - acum_t[h:h + 1, :]
                    dec = jnp.exp(jnp.where(causal, seg, NEG_INF))
                    mats.append((cbm * dec * dt_t[h:h + 1, :]).astype(BF16))
                yd = _dot(jnp.concatenate(mats, axis=0), xs_b[:, h0 * HEAD_DIM:h0 * HEAD_DIM + LANE])
                lt = slice(hp * LANE, (hp + 1) * LANE)
                y_tiles.append(jnp.where(lo_half, yd[0:CHUNK], yd[CHUNK:2 * CHUNK]) + y_off[:, lt])
                between()
        between()
        y = jnp.concatenate(y_tiles, axis=1) + xs * dsk_ref[...]
        y = y * jax.nn.silu(src_ref[r0:r0 + CHUNK, 0:SSD_WIDTH].astype(F32))
        for g in range(SSD_GROUPS):
            yg = y[:, g * gw:(g + 1) * gw]
            ms = jnp.mean(yg * yg, axis=-1, keepdims=True)
            y_ref[r0:r0 + CHUNK, g * gw:(g + 1) * gw] = (
                yg * lax.rsqrt(ms + EPS) * gn_ref[:, g * gw:(g + 1) * gw]).astype(BF16)
        between()

    xpad_ref[0:SUBLANE, :] = xpad_ref[ts:ts + SUBLANE, :]


def _att_kernel(q_ref, kp_ref, kc_ref, vp_ref, vc_ref, o_ref, m_ref, d_ref, *, tq):
    n = pl.program_id(2)
    blk = ATT_BLOCK
    iq = lax.broadcasted_iota(jnp.int32, (2 * blk, 2 * blk), 0) & (blk - 1)
    ik = lax.broadcasted_iota(jnp.int32, (2 * blk, 2 * blk), 1)
    delta = ik - iq
    band = (delta >= 0) & (delta <= blk)
    bias = jnp.where(band, 0.0, NEG_INF)
    before_start = jnp.where(n > 0, 0.0, NEG_INF)
    bias_first = jnp.where(ik >= blk, bias, jnp.minimum(bias, before_start))
    lane = lax.broadcasted_iota(jnp.int32, (blk, LANE), 1)
    lo_half = lane < HEAD_DIM
    head_of_lane = lane // STAT_LANES

    for i in range(tq // blk):
        rs = slice(i * blk, (i + 1) * blk)
        m_c = jnp.zeros((blk, LANE), F32)
        d_c = jnp.zeros((blk, LANE), F32)
        for j in range(ATT_WIDTH // LANE):
            cs = slice(j * LANE, (j + 1) * LANE)
            q2 = q_ref[rs, cs]
            zero = jnp.zeros_like(q2)
            qq = jnp.concatenate([jnp.where(lo_half, q2, zero), jnp.where(lo_half, zero, q2)], axis=0)
            if i == 0:
                k2 = jnp.concatenate([kp_ref[:, cs], kc_ref[0:blk, cs]], axis=0)
                v2 = jnp.concatenate([vp_ref[:, cs], vc_ref[0:blk, cs]], axis=0)
            else:
                k2 = kc_ref[(i - 1) * blk:(i + 1) * blk, cs]
                v2 = vc_ref[(i - 1) * blk:(i + 1) * blk, cs]
            s = _nt_dot(qq, k2) + (bias_first if i == 0 else bias)
            m = jnp.max(s, axis=-1, keepdims=True)
            p = jnp.exp2(s - m)
            den = jnp.sum(p, axis=-1, keepdims=True)
            pv = _dot(p.astype(BF16), v2)
            o_ref[rs, cs] = jnp.where(lo_half, pv[0:blk], pv[blk:2 * blk]).astype(BF16)
            for hh in range(2):
                sel = head_of_lane == 2 * j + hh
                m_c = jnp.where(sel, m[hh * blk:(hh + 1) * blk], m_c)
                d_c = jnp.where(sel, den[hh * blk:(hh + 1) * blk], d_c)
        m_ref[rs, :] = m_c
        d_ref[rs, :] = d_c


def _att_branch(qk_arr, v_arr, v_blk, tq):
    batch, dil, sub, _ = qk_arr.shape
    kb = tq // ATT_BLOCK
    tile = lambda blk: pl.BlockSpec((None, None, tq, PB), lambda b, r, n: (b, r, n, blk))
    stat = pl.BlockSpec((None, None, tq, LANE), lambda b, r, n: (b, r, n, 0))
    prev = lambda blk: pl.BlockSpec((None, None, ATT_BLOCK, PB),
                                    lambda b, r, n: (b, r, jnp.maximum(n * kb - 1, 0), blk))
    return pl.pallas_call(
        functools.partial(_att_kernel, tq=tq),
        grid=(batch, dil, sub // tq),
        in_specs=[tile(0), prev(1), tile(1), prev(v_blk), tile(v_blk)],
        out_specs=[tile(0), stat, stat],
        out_shape=[_sds((batch, dil, sub, ATT_WIDTH), BF16), _sds((batch, dil, sub, LANE), F32),
                   _sds((batch, dil, sub, LANE), F32)],
        compiler_params=_cparams(("parallel", "parallel", "arbitrary")),
        name=f"att_d{dil}",
    )(qk_arr, qk_arr, qk_arr, v_arr, v_arr)


def _ret_tile(q_ref, k_ref, v_ref, g_ref, gn_ref, y_ref, st_ref, *, ts):
    row_i = lax.broadcasted_iota(jnp.int32, (CHUNK, LANE), 0)
    col_i = lax.broadcasted_iota(jnp.int32, (CHUNK, LANE), 1)
    row = row_i.astype(F32)
    rel = (row_i - col_i).astype(F32)
    lo_half = col_i < RET_QK_DIM
    log_gamma = [math.log1p(-(2.0 ** (-5.0 - h))) for h in range(RET_HEADS)]
    decay = [jnp.where(rel >= 0, jnp.exp(jnp.maximum(rel, 0.0) * lg), 0.0) for lg in log_gamma]
    lg_lane = [jnp.where(lo_half, log_gamma[2 * jp], log_gamma[2 * jp + 1]) for jp in range(RET_HEADS // 2)]
    to_end = [jnp.exp((CHUNK - 1.0 - row) * lg) for lg in lg_lane]
    from_start = [jnp.exp((row + 1.0) * lg) for lg in lg_lane]

    for c in range(ts // CHUNK):
        rs = slice(c * CHUNK, (c + 1) * CHUNK)
        for jp in range(RET_HEADS // 2):
            cs = slice(jp * LANE, (jp + 1) * LANE)
            q2 = q_ref[rs, cs].astype(F32)
            k2 = k_ref[rs, cs].astype(F32)
            zero = jnp.zeros_like(q2)
            qa = jnp.where(lo_half, q2, zero)
            qb = jnp.where(lo_half, zero, q2)
            scores = _nt_dot(jnp.concatenate([qa, qb], axis=0).astype(BF16), k2.astype(BF16))
            k_end = (k2 * to_end[jp]).astype(BF16)
            st_pair = st_ref[jp * LANE:(jp + 1) * LANE, :]
            st_b = st_pair.astype(BF16)
            new_rows = []
            for hh, qh in enumerate((qa, qb)):
                h = 2 * jp + hh
                vh = v_ref[rs, h * RET_V_DIM:(h + 1) * RET_V_DIM]
                inner = _dot((scores[hh * CHUNK:(hh + 1) * CHUNK] * decay[h]).astype(BF16), vh)
                cross = _dot((qh * from_start[jp]).astype(BF16), st_b)
                kv = _tn_dot(k_end, vh)
                half = slice(hh * RET_QK_DIM, (hh + 1) * RET_QK_DIM)
                new_rows.append(st_pair[half] * math.exp(CHUNK * log_gamma[h]) + kv[half])
                yh = inner + cross
                ms = jnp.mean(yh * yh, axis=-1, keepdims=True)
                hs = slice(h * RET_V_DIM, (h + 1) * RET_V_DIM)
                y_ref[rs, hs] = (yh * lax.rsqrt(ms + EPS) * gn_ref[:, hs]
                                 * jax.nn.silu(g_ref[rs, hs].astype(F32))).astype(BF16)
            st_ref[jp * LANE:(jp + 1) * LANE, :] = jnp.concatenate(new_rows, axis=0)


def _outproj_kernel(ys_ref, rq_ref, rk_ref, rv_ref, rg_ref, rgn_ref, x_ref, w_ref, g_ref, *rest,
                    tm, dils, tiles_per_seq):
    nb = len(dils)
    o_refs, m_refs, d_refs = rest[0:nb], rest[nb:2 * nb], rest[2 * nb:3 * nb]
    xo_ref, hn_ref, il_ref, st_ref, ya_ref, yr_ref, rst_ref = rest[3 * nb:]

    @pl.when(pl.program_id(0) % tiles_per_seq == 0)
    def _():
        rst_ref[...] = jnp.zeros(rst_ref.shape, F32)

    acc = x_ref[...] + _dot(ys_ref[...], w_ref[0:SSD_WIDTH, :])
    _ret_tile(rq_ref, rk_ref, rv_ref, rg_ref, rgn_ref, yr_ref, rst_ref, ts=tm)
    maxes, sums = [], []
    for k, dil in enumerate(dils):
        if dil == 1:
            maxes.append(m_refs[k][0])
            sums.append(d_refs[k][0])
            continue
        for res in range(dil):
            rows = pl.ds(res, tm // dil, stride=dil)
            st_ref[2 * k, rows, :] = m_refs[k][res]
            st_ref[2 * k + 1, rows, :] = d_refs[k][res]
        maxes.append(st_ref[2 * k])
        sums.append(st_ref[2 * k + 1])
    m_all = functools.reduce(jnp.maximum, maxes)
    es = [jnp.exp2(m - m_all) for m in maxes]
    inv_z = 1.0 / functools.reduce(lambda a, b: a + b, [e * d for e, d in zip(es, sums)])
    wts = [e * inv_z for e in es]
    lo_half = lax.broadcasted_iota(jnp.int32, (tm, LANE), 1) < HEAD_DIM
    for c in range(ATT_WIDTH // LANE):
        cs = slice(c * LANE, (c + 1) * LANE)
        la, lb = 2 * c * STAT_LANES, (2 * c + 1) * STAT_LANES
        y_att = None
        for k, dil in enumerate(dils):
            if dil == 1:
                o = o_refs[k][0, :, cs].astype(F32)
            else:
                for res in range(dil):
                    il_ref[k, c, pl.ds(res, tm // dil, stride=dil), :] = o_refs[k][res, :, cs].astype(F32)
                o = il_ref[k, c]
            wk = jnp.where(lo_half, jnp.broadcast_to(wts[k][:, la:la + 1], (tm, LANE)),
                           jnp.broadcast_to(wts[k][:, lb:lb + 1], (tm, LANE)))
            y_att = wk * o if y_att is None else y_att + wk * o
        ya_ref[:, cs] = y_att.astype(BF16)
    acc = acc + _dot(ya_ref[...], w_ref[SSD_WIDTH:SSD_WIDTH + ATT_WIDTH, :])
    acc = acc + _dot(yr_ref[...], w_ref[SSD_WIDTH + ATT_WIDTH:MIX_WIDTH, :])
    xo_ref[...] = acc
    ms = jnp.mean(acc * acc, axis=-1, keepdims=True)
    hn_ref[...] = ((acc * lax.rsqrt(ms + EPS)) * g_ref[...]).astype(BF16)


def _outproj(y_ssd, outs, maxes, sums, qkr, proj, ret_gain, x2d, w_out, gain, seq, tm):
    t = x2d.shape[0]
    nseq = seq // tm
    dils = tuple(o.shape[1] for o in outs)
    rows = lambda width, blk=0: pl.BlockSpec((tm, width), lambda i: (i, blk))
    branch = lambda dil, width: pl.BlockSpec((None, dil, tm // dil, width), lambda i: (i // nseq, 0, i % nseq, 0))
    ret_blk = 2 * PB // RET_QK_WIDTH
    return pl.pallas_call(
        functools.partial(_outproj_kernel, tm=tm, dils=dils, tiles_per_seq=nseq),
        grid=(t // tm,),
        in_specs=[rows(SSD_WIDTH), rows(RET_QK_WIDTH, ret_blk), rows(RET_QK_WIDTH, ret_blk + 1),
                  rows(PB, BLK_RV), rows(PB, BLK_RG), pl.BlockSpec((1, RET_V_WIDTH), lambda i: (0, 0)),
                  rows(D_MODEL),
                  pl.BlockSpec((MIX_WIDTH, D_MODEL), lambda i: (0, 0), pipeline_mode=pl.Buffered(1)),
                  pl.BlockSpec((1, D_MODEL), lambda i: (0, 0))]
                 + [branch(d, ATT_WIDTH) for d in dils] + [branch(d, LANE) for d in dils] * 2,
        out_specs=[rows(D_MODEL), rows(D_MODEL)],
        out_shape=[_sds((t, D_MODEL), F32), _sds((t, D_MODEL), BF16)],
        scratch_shapes=[pltpu.VMEM((len(dils), ATT_WIDTH // LANE, tm, LANE), F32),
                        pltpu.VMEM((2 * len(dils), tm, LANE), F32),
                        pltpu.VMEM((tm, ATT_WIDTH), BF16),
                        pltpu.VMEM((tm, RET_V_WIDTH), BF16),
                        pltpu.VMEM((RET_QK_WIDTH, RET_V_DIM), F32)],
        compiler_params=_cparams(("arbitrary",)),
        name="outproj",
    )(y_ssd, qkr, qkr, proj, proj, ret_gain, x2d, w_out, gain, *outs, *maxes, *sums)


def _ffn_kernel(hn_ref, x_ref, wg_ref, wu_ref, wd_ref, o_ref):
    @pl.when(pl.program_id(1) == 0)
    def _():
        o_ref[...] = x_ref[...]

    h = hn_ref[...]
    act = (jax.nn.silu(_dot(h, wg_ref[...])) * _dot(h, wu_ref[...])).astype(BF16)
    o_ref[...] += _dot(act, wd_ref[...])


def _ffn(hn, x2d, w_gate, w_up, w_down, tm, tf):
    t = x2d.shape[0]
    return pl.pallas_call(
        _ffn_kernel,
        grid=(t // tm, D_FF // tf),
        in_specs=[
            pl.BlockSpec((tm, D_MODEL), lambda i, j: (i, 0)),
            pl.BlockSpec((tm, D_MODEL), lambda i, j: (i, 0)),
            pl.BlockSpec((D_MODEL, tf), lambda i, j: (0, j)),
            pl.BlockSpec((D_MODEL, tf), lambda i, j: (0, j)),
            pl.BlockSpec((tf, D_MODEL), lambda i, j: (j, 0)),
        ],
        out_specs=pl.BlockSpec((tm, D_MODEL), lambda i, j: (i, 0)),
        out_shape=_sds((t, D_MODEL), F32),
        compiler_params=_cparams(("parallel", "arbitrary")),
        name="ffn",
    )(hn, x2d, w_gate, w_up, w_down)


def _tiles(seq):
    return dict(
        inproj=min(512, seq),
        prep=min(512, seq),
        att=512,
        outproj=min(512, seq),
        ffn=(min(512, seq), 512),
    )


def _rope_tables(seq):
    pos = jnp.arange(seq, dtype=F32)
    inv = ROPE_THETA ** (-jnp.arange(0, HEAD_DIM, 2, dtype=F32) / HEAD_DIM)
    ang = pos[:, None] * inv[None, :]
    cos, sin = jnp.cos(ang), jnp.sin(ang)
    cosf = jnp.concatenate([cos, cos, cos, cos], axis=-1)
    sinf = jnp.concatenate([-sin, sin, -sin, sin], axis=-1)
    return cosf, sinf


def _head_sum_matrix():
    idx = np.arange(PB) // HEAD_DIM
    return jnp.asarray(idx[:, None] == idx[None, :], dtype=BF16)


def kernel(x, ln_mix, w_in, conv_w, conv_b, dt_bias, a_log, d_skip, ssd_norm, q_norm, k_norm, ret_norm, w_out,
           ln_ffn, w_gate, w_up, w_down):
    batch, seq, _ = x.shape
    depth = w_in.shape[0]
    t = batch * seq
    tiles = _tiles(seq)
    cosf, sinf = _rope_tables(seq)
    gmat = _head_sum_matrix()
    lane_pad = lambda v: jnp.pad(v.astype(F32), (0, LANE - v.shape[0]))[None, :]
    dils = tuple(dil for _, dil in DILATED_PAIRS)
    assert dils[0] == 1 and all(window // dil == ATT_BLOCK for window, dil in DILATED_PAIRS)

    x2d = x.reshape(t, D_MODEL)
    for i in range(depth):
        w_i = w_in[i]
        w_main = jnp.concatenate([w_i[:, :DT_COL], w_i[:, DT_COL + SSD_HEADS:]], axis=1).astype(BF16)
        w_dt = jnp.pad(w_i[:, DT_COL:DT_COL + SSD_HEADS], ((0, 0), (0, LANE - SSD_HEADS))).astype(BF16)
        proj, y_ssd = _inproj_ssd(x2d, ln_mix[i][None, :], w_main, w_dt, conv_w[i], conv_b[i][None, :],
                                  lane_pad(dt_bias[i]), lane_pad(a_log[i]), jnp.repeat(d_skip[i], SSD_HEAD_DIM)[None, :],
                                  ssd_norm[i][None, :], seq, tiles["inproj"])

        qkr, *qkv_dil = _prep(proj, cosf, sinf, jnp.tile(q_norm[i], ATT_HEADS)[None, :],
                              jnp.tile(k_norm[i], ATT_HEADS)[None, :], gmat, batch, seq, tiles["prep"], dils[1:])
        branch_in = [(qkr.reshape(batch, 1, seq, QKR_WIDTH), proj.reshape(batch, 1, seq, IN_MAIN), BLK_AV)]
        branch_in += [(a, a, 2) for a in qkv_dil]
        outs, maxes, sums = zip(*[_att_branch(qk_arr, v_arr, v_blk, min(tiles["att"], qk_arr.shape[2]))
                                  for qk_arr, v_arr, v_blk in branch_in])
        x2d, hn = _outproj(y_ssd, outs, maxes, sums, qkr, proj, ret_norm[i][None, :], x2d, w_out[i].astype(BF16),
                           ln_ffn[i][None, :], seq, tiles["outproj"])
        x2d = _ffn(hn, x2d, w_gate[i].astype(BF16), w_up[i].astype(BF16), w_down[i].astype(BF16), *tiles["ffn"])
    return x2d.reshape(batch, seq, D_MODEL)
```
